```python
import jax, jax.numpy as jnp
from jax import lax
import numpy as np

D_MODEL = 4096
BATCH = 4
SEQ = 2048
DEPTH = 2

D_MIX = D_MODEL
RET_HEADS = 16
D_RET = D_MIX // 2
RET_HEAD_DIM = D_RET // RET_HEADS
D_CONV = D_MIX - D_RET
CONV_GROUPS = 16
CONV_WIDTH = 3
D_FF = 11008
CHUNK = 128
ROPE_THETA = 10000.0
EPS = 1e-6
D_IN_PROJ = 4 * D_RET + 3 * D_CONV

kernel_name = "hybrid_retention_shortconv_macaron"


def rmsnorm(x, g):
    xf = x.astype(jnp.float32)
    y = xf * lax.rsqrt(jnp.mean(xf * xf, axis=-1, keepdims=True) + EPS)
    return (y * g.astype(jnp.float32)).astype(x.dtype)


def swiglu(x, w_gate, w_up, w_down):
    return (jax.nn.silu(x @ w_gate) * (x @ w_up)) @ w_down


def rope(x):
    s, dh = x.shape[1], x.shape[-1]
    half = dh // 2
    inv_freq = ROPE_THETA ** (-jnp.arange(half, dtype=jnp.float32) / half)
    ang = jnp.arange(s, dtype=jnp.float32)[:, None] * inv_freq[None, :]
    cos = jnp.cos(ang)[None, :, None, :]
    sin = jnp.sin(ang)[None, :, None, :]
    x1, x2 = x[..., :half], x[..., half:]
    return jnp.concatenate([x1 * cos - x2 * sin, x1 * sin + x2 * cos], axis=-1)


def retention_chunkwise(q, k, v):
    b, s, h, dh = q.shape
    n = s // CHUNK
    log_gamma = jnp.log1p(-jnp.power(2.0, -5.0 - jnp.arange(h, dtype=jnp.float32)))

    def to_chunks(t):
        return t.reshape(b, n, CHUNK, h, dh).transpose(0, 3, 1, 2, 4)

    qc, kc, vc = to_chunks(q), to_chunks(k), to_chunks(v)
    idx = jnp.arange(CHUNK, dtype=jnp.float32)
    diff = idx[:, None] - idx[None, :]
    intra_decay = jnp.where(diff[None] >= 0,
                            jnp.exp(log_gamma[:, None, None] * jnp.maximum(diff, 0.0)[None]),
                            0.0)
    zeta = jnp.exp(log_gamma[:, None] * (CHUNK - 1.0 - idx)[None, :])
    xi = jnp.exp(log_gamma[:, None] * (idx + 1.0)[None, :])
    chunk_decay = jnp.exp(log_gamma * CHUNK)

    scores = jnp.einsum('bhncd,bhnmd->bhncm', qc, kc) * intra_decay[None, :, None]
    out_intra = jnp.einsum('bhncm,bhnme->bhnce', scores, vc)

    kv_chunk = jnp.einsum('bhnmd,bhnme->bhnde', kc * zeta[None, :, None, :, None], vc)

    def step(state, kv_n):
        return state * chunk_decay[None, :, None, None] + kv_n, state

    _, prev_states = lax.scan(step, jnp.zeros((b, h, dh, dh), jnp.float32),
                              jnp.moveaxis(kv_chunk, 2, 0))
    prev_states = jnp.moveaxis(prev_states, 0, 2)
    out_inter = jnp.einsum('bhncd,bhnde->bhnce', qc, prev_states) * xi[None, :, None, :, None]

    out = out_intra + out_inter
    return out.transpose(0, 2, 3, 1, 4).reshape(b, s, h, dh)


def causal_depthwise_conv3(u, w):
    s = u.shape[1]
    up = jnp.pad(u, ((0, 0), (CONV_WIDTH - 1, 0), (0, 0)))
    return sum(w[j][None, None, :] * up[:, j:j + s, :] for j in range(CONV_WIDTH))


def hybrid_mixer(h, w_in, conv_w, ret_norm, w_out):
    b, s, _ = h.shape
    proj = h @ w_in
    splits = [D_RET, 2 * D_RET, 3 * D_RET, 4 * D_RET, 4 * D_RET + D_CONV, 4 * D_RET + 2 * D_CONV]
    q, k, v, g, gate_b, gate_c, u = jnp.split(proj, splits, axis=-1)

    shp = (b, s, RET_HEADS, RET_HEAD_DIM)
    qf = rope(q.reshape(shp).astype(jnp.float32))
    kf = rope(k.reshape(shp).astype(jnp.float32)) * (RET_HEAD_DIM ** -0.5)
    vf = v.reshape(shp).astype(jnp.float32)
    ret = retention_chunkwise(qf, kf, vf)
    ret = ret * lax.rsqrt(jnp.mean(ret * ret, axis=-1, keepdims=True) + EPS)
    ret = ret.reshape(b, s, D_RET) * ret_norm.astype(jnp.float32)
    ret_out = (jax.nn.silu(g.astype(jnp.float32)) * ret).astype(h.dtype)

    conv_out = gate_b * causal_depthwise_conv3(gate_c * u, conv_w)

    return jnp.concatenate([ret_out, conv_out], axis=-1) @ w_out


def setup_inputs(seed: int = 0) -> dict:
    key = jax.random.key(seed)
    ks = jax.random.split(key, 16)
    f32 = jnp.float32

    def nrm(k, shape, scale):
        return jax.random.normal(k, shape, f32) * scale

    def gain(k, shape):
        return 1.0 + 0.02 * jax.random.normal(k, shape, f32)

    return {
        "x": nrm(ks[0], (BATCH, SEQ, D_MODEL), 1.0),
        "norm_ffa": gain(ks[1], (DEPTH, D_MODEL)),
        "w_ffa_gate": nrm(ks[2], (DEPTH, D_MODEL, D_FF), D_MODEL ** -0.5),
        "w_ffa_up": nrm(ks[3], (DEPTH, D_MODEL, D_FF), D_MODEL ** -0.5),
        "w_ffa_down": nrm(ks[4], (DEPTH, D_FF, D_MODEL), D_FF ** -0.5),
        "norm_mix": gain(ks[5], (DEPTH, D_MODEL)),
        "w_in": nrm(ks[6], (DEPTH, D_MODEL, D_IN_PROJ), D_MODEL ** -0.5),
        "conv_w": nrm(ks[7], (DEPTH, CONV_WIDTH, D_CONV), CONV_WIDTH ** -0.5),
        "ret_norm": gain(ks[8], (DEPTH, D_RET)),
        "w_out": nrm(ks[9], (DEPTH, D_MIX, D_MODEL), D_MIX ** -0.5),
        "norm_ffb": gain(ks[10], (DEPTH, D_MODEL)),
        "w_ffb_gate": nrm(ks[11], (DEPTH, D_MODEL, D_FF), D_MODEL ** -0.5),
        "w_ffb_up": nrm(ks[12], (DEPTH, D_MODEL, D_FF), D_MODEL ** -0.5),
        "w_ffb_down": nrm(ks[13], (DEPTH, D_FF, D_MODEL), D_FF ** -0.5),
        "norm_final": gain(ks[14], (D_MODEL,)),
    }


def reference(x, norm_ffa, w_ffa_gate, w_ffa_up, w_ffa_down, norm_mix, w_in, conv_w,
              ret_norm, w_out, norm_ffb, w_ffb_gate, w_ffb_up, w_ffb_down, norm_final):
    for l in range(DEPTH):
        x = x + 0.5 * swiglu(rmsnorm(x, norm_ffa[l]), w_ffa_gate[l], w_ffa_up[l], w_ffa_down[l])
        x = x + hybrid_mixer(rmsnorm(x, norm_mix[l]), w_in[l], conv_w[l], ret_norm[l], w_out[l])
        x = x + 0.5 * swiglu(rmsnorm(x, norm_ffb[l]), w_ffb_gate[l], w_ffb_up[l], w_ffb_down[l])
    return rmsnorm(x, norm_final)
```

```python
import functools

import jax
import jax.numpy as jnp
from jax import lax
from jax.experimental import pallas as pl
from jax.experimental.pallas import tpu as pltpu

F32 = jnp.float32
BF16 = jnp.bfloat16

RET_HEADS = 16
HEAD_DIM = 128
CONV_GROUP = 128
CONV_WIDTH = 3
CHUNK = 128
ROPE_THETA = 10000.0
EPS = 1e-6

V7X_VMEM_LIMIT_BYTES = 56 * 1024 * 1024


def _params(n_grid_axes):
    return pltpu.CompilerParams(
        dimension_semantics=("arbitrary",) * n_grid_axes,
        vmem_limit_bytes=V7X_VMEM_LIMIT_BYTES,
    )


def _rmsnorm_body(x_ref, g_ref, o_ref):
    x = x_ref[...]
    y = x * lax.rsqrt(jnp.mean(x * x, axis=-1, keepdims=True) + EPS)
    o_ref[...] = (y * g_ref[...]).astype(o_ref.dtype)


def _rmsnorm(x, gain, out_dtype, block_rows=512):
    t, d = x.shape
    return pl.pallas_call(
        _rmsnorm_body,
        grid=(t // block_rows,),
        in_specs=[
            pl.BlockSpec((block_rows, d), lambda i: (i, 0)),
            pl.BlockSpec((1, d), lambda i: (0, 0)),
        ],
        out_specs=pl.BlockSpec((block_rows, d), lambda i: (i, 0)),
        out_shape=jax.ShapeDtypeStruct((t, d), out_dtype),
        compiler_params=_params(1),
        name="rmsnorm",
    )(x, gain.reshape(1, d))


def _dot(a, w_f32):
    return jnp.dot(a, w_f32.astype(BF16), preferred_element_type=F32)


def _gate_up_body(a_ref, wg_ref, wu_ref, o_ref):
    a = a_ref[...]
    gate = _dot(a, wg_ref[...])
    up = _dot(a, wu_ref[...])
    o_ref[...] = (jax.nn.silu(gate) * up).astype(o_ref.dtype)


def _gate_up(a, w_gate, w_up, layer, bm=1024, bn=256):
    t, d = a.shape
    ff = w_gate.shape[2]
    return pl.pallas_call(
        _gate_up_body,
        grid=(t // bm, ff // bn),
        in_specs=[
            pl.BlockSpec((bm, d), lambda i, j: (i, 0)),
            pl.BlockSpec((None, d, bn), lambda i, j: (layer, 0, j)),
            pl.BlockSpec((None, d, bn), lambda i, j: (layer, 0, j)),
        ],
        out_specs=pl.BlockSpec((bm, bn), lambda i, j: (i, j)),
        out_shape=jax.ShapeDtypeStruct((t, ff), BF16),
        compiler_params=_params(2),
        name="ffn_gate_up",
    )(a, w_gate, w_up)


def _down_body(a_ref, w_ref, r_ref, o_ref, acc_ref):
    k = pl.program_id(2)
    part = _dot(a_ref[...], w_ref[...])

    @pl.when(k == 0)
    def _():
        acc_ref[...] = part

    @pl.when(k == pl.num_programs(2) - 1)
    def _():
        o_ref[...] = r_ref[...] + 0.5 * (acc_ref[...] + part)


def _down(a, w, resid, layer, bm=1024, bn=512, k_steps=2):
    t, ff = a.shape
    d = w.shape[2]
    bk = ff // k_steps
    assert k_steps == 2 and bk * k_steps == ff
    return pl.pallas_call(
        _down_body,
        grid=(t // bm, d // bn, k_steps),
        in_specs=[
            pl.BlockSpec((bm, bk), lambda i, j, k: (i, k)),
            pl.BlockSpec((None, bk, bn), lambda i, j, k: (layer, k, j)),
            pl.BlockSpec((bm, bn), lambda i, j, k: (i, j)),
        ],
        out_specs=pl.BlockSpec((bm, bn), lambda i, j, k: (i, j)),
        out_shape=jax.ShapeDtypeStruct((t, d), F32),
        scratch_shapes=[pltpu.VMEM((bm, bn), F32)],
        compiler_params=_params(3),
        name="ffn_down",
    )(a, w, resid)


def _in_proj_body(a_ref, w_ref, o_ref):
    o_ref[...] = _dot(a_ref[...], w_ref[...])


def _in_proj(a, w, layer, bm=1024, bn=512):
    t, d = a.shape
    n = w.shape[2]
    return pl.pallas_call(
        _in_proj_body,
        grid=(t // bm, n // bn),
        in_specs=[
            pl.BlockSpec((bm, d), lambda i, j: (i, 0)),
            pl.BlockSpec((None, d, bn), lambda i, j: (layer, 0, j)),
        ],
        out_specs=pl.BlockSpec((bm, bn), lambda i, j: (i, j)),
        out_shape=jax.ShapeDtypeStruct((t, n), F32),
        compiler_params=_params(2),
        name="mixer_in_proj",
    )(a, w)


def _out_proj_body(a1_ref, a2_ref, w1_ref, w2_ref, r_ref, o_ref):
    acc = _dot(a1_ref[...], w1_ref[...]) + _dot(a2_ref[...], w2_ref[...])
    o_ref[...] = r_ref[...] + acc


def _out_proj(a_ret, a_conv, w, resid, layer, bm=1024, bn=512):
    t, d_ret = a_ret.shape
    d_conv = a_conv.shape[1]
    assert d_ret == d_conv
    n = w.shape[2]
    return pl.pallas_call(
        _out_proj_body,
        grid=(t // bm, n // bn),
        in_specs=[
            pl.BlockSpec((bm, d_ret), lambda i, j: (i, 0)),
            pl.BlockSpec((bm, d_conv), lambda i, j: (i, 0)),
            pl.BlockSpec((None, d_ret, bn), lambda i, j: (layer, 0, j)),
            pl.BlockSpec((None, d_conv, bn), lambda i, j: (layer, 1, j)),
            pl.BlockSpec((bm, bn), lambda i, j: (i, j)),
        ],
        out_specs=pl.BlockSpec((bm, bn), lambda i, j: (i, j)),
        out_shape=jax.ShapeDtypeStruct((t, n), F32),
        compiler_params=_params(2),
        name="mixer_out_proj",
    )(a_ret, a_conv, w, w, resid)


def _retention_body(q_ref, k_ref, v_ref, g_ref, cos_ref, sin_ref, lg_ref, gn_ref, o_ref):
    s, dh = q_ref.shape
    n_chunks = s // CHUNK
    lg = lg_ref[...]

    row = lax.broadcasted_iota(jnp.int32, (CHUNK, CHUNK), 0)
    col = lax.broadcasted_iota(jnp.int32, (CHUNK, CHUNK), 1)
    diff = (row - col).astype(F32)
    intra_decay = jnp.where(diff >= 0, jnp.exp(lg * jnp.maximum(diff, 0.0)), 0.0)
    idx = lax.broadcasted_iota(jnp.int32, (CHUNK, dh), 0).astype(F32)
    zeta = jnp.exp(lg * (CHUNK - 1.0 - idx))
    xi = jnp.exp(lg * (idx + 1.0))
    chunk_decay = jnp.exp(lg * float(CHUNK))

    def rope(x, cos, sin):
        return x * cos + pltpu.roll(x, dh // 2, 1) * sin

    state = jnp.zeros((dh, dh), F32)
    for c in range(n_chunks):
        rows = pl.ds(c * CHUNK, CHUNK)
        cos = cos_ref[rows, :]
        sin = sin_ref[rows, :]
        q = rope(q_ref[rows, :], cos, sin)
        k = rope(k_ref[rows, :], cos, sin) * (dh ** -0.5)
        v = v_ref[rows, :]
        q16 = q.astype(BF16)
        v16 = v.astype(BF16)

        scores = lax.dot_general(q16, k.astype(BF16), (((1,), (1,)), ((), ())),
                                 preferred_element_type=F32) * intra_decay
        out = jnp.dot(scores.astype(BF16), v16, preferred_element_type=F32)
        out = out + jnp.dot(q16, state.astype(BF16), preferred_element_type=F32) * xi
        kv = lax.dot_general((k * zeta).astype(BF16), v16, (((0,), (0,)), ((), ())),
                             preferred_element_type=F32)
        state = state * chunk_decay + kv

        out = out * lax.rsqrt(jnp.mean(out * out, axis=-1, keepdims=True) + EPS)
        out = out * gn_ref[...]
        o_ref[rows, :] = (jax.nn.silu(g_ref[rows, :]) * out).astype(o_ref.dtype)


def _rope_tables(seq, dh):
    half = dh // 2
    inv_freq = ROPE_THETA ** (-jnp.arange(half, dtype=F32) / half)
    ang = jnp.arange(seq, dtype=F32)[:, None] * inv_freq[None, :]
    cos = jnp.cos(ang)
    sin = jnp.sin(ang)
    return jnp.concatenate([cos, cos], axis=-1), jnp.concatenate([-sin, sin], axis=-1)


def _retention(proj, ret_norm):
    b, s, _ = proj.shape
    h, dh = RET_HEADS, HEAD_DIM
    cos, sin = _rope_tables(s, dh)
    log_gamma = jnp.log1p(-jnp.power(2.0, -5.0 - jnp.arange(h, dtype=F32)))
    lg = jnp.broadcast_to(log_gamma[:, None, None], (h, 1, dh))
    gn = ret_norm.astype(F32).reshape(h, 1, dh)

    def head_cols(group):
        return pl.BlockSpec((None, s, dh), lambda bi, hi: (bi, 0, group * h + hi))

    table = pl.BlockSpec((s, dh), lambda bi, hi: (0, 0))
    per_head = pl.BlockSpec((None, 1, dh), lambda bi, hi: (hi, 0, 0))
    return pl.pallas_call(
        _retention_body,
        grid=(b, h),
        in_specs=[head_cols(0), head_cols(1), head_cols(2), head_cols(3),
                  table, table, per_head, per_head],
        out_specs=pl.BlockSpec((None, s, dh), lambda bi, hi: (bi, 0, hi)),
        out_shape=jax.ShapeDtypeStruct((b, s, h * dh), BF16),
        compiler_params=_params(2),
        name="retention",
    )(proj, proj, proj, proj, cos, sin, lg, gn)


def _short_conv_body(b_ref, c_ref, u_ref, w_ref, o_ref):
    cu = c_ref[...] * u_ref[...]
    t = lax.broadcasted_iota(jnp.int32, cu.shape, 0)
    acc = w_ref[CONV_WIDTH - 1:CONV_WIDTH, :] * cu
    for lag in range(1, CONV_WIDTH):
        shifted = jnp.where(t >= lag, pltpu.roll(cu, lag, 0), 0.0)
        acc = acc + w_ref[CONV_WIDTH - 1 - lag:CONV_WIDTH - lag, :] * shifted
    o_ref[...] = (b_ref[...] * acc).astype(o_ref.dtype)


def _short_conv(proj, conv_w):
    b, s, _ = proj.shape
    d_conv = conv_w.shape[1]
    groups = d_conv // CONV_GROUP
    first = 4 * RET_HEADS * HEAD_DIM // CONV_GROUP

    def group_cols(which):
        return pl.BlockSpec((None, s, CONV_GROUP),
                            lambda bi, gi: (bi, 0, first + which * groups + gi))

    return pl.pallas_call(
        _short_conv_body,
        grid=(b, groups),
        in_specs=[group_cols(0), group_cols(1), group_cols(2),
                  pl.BlockSpec((CONV_WIDTH, CONV_GROUP), lambda bi, gi: (0, gi))],
        out_specs=pl.BlockSpec((None, s, CONV_GROUP), lambda bi, gi: (bi, 0, gi)),
        out_shape=jax.ShapeDtypeStruct((b, s, d_conv), BF16),
        compiler_params=_params(2),
        name="short_conv",
    )(proj, proj, proj, conv_w)


def _swiglu_residual(x, norm_gain, w_gate, w_up, w_down, layer):
    hidden = _gate_up(_rmsnorm(x, norm_gain, BF16), w_gate, w_up, layer)
    return _down(hidden, w_down, x, layer)


def kernel(x, norm_ffa, w_ffa_gate, w_ffa_up, w_ffa_down, norm_mix, w_in, conv_w, ret_norm,
           w_out, norm_ffb, w_ffb_gate, w_ffb_up, w_ffb_down, norm_final):
    batch, seq, d_model = x.shape
    depth = norm_ffa.shape[0]
    x = x.reshape(batch * seq, d_model)
    for l in range(depth):
        x = _swiglu_residual(x, norm_ffa[l], w_ffa_gate, w_ffa_up, w_ffa_down, l)
        proj = _in_proj(_rmsnorm(x, norm_mix[l], BF16), w_in, l).reshape(batch, seq, -1)
        ret_out = _retention(proj, ret_norm[l]).reshape(batch * seq, -1)
        conv_out = _short_conv(proj, conv_w[l]).reshape(batch * seq, -1)
        x = _out_proj(ret_out, conv_out, w_out, x, l)
        x = _swiglu_residual(x, norm_ffb[l], w_ffb_gate, w_ffb_up, w_ffb_down, l)
    return _rmsnorm(x, norm_final, F32).reshape(batch, seq, d_model)
```

```python
import functools

import jax
import jax.numpy as jnp
from jax import lax
from jax.experimental import pallas as pl
from jax.experimental.pallas import tpu as pltpu

F32 = jnp.float32
BF16 = jnp.bfloat16

RET_HEADS = 16
HEAD_DIM = 128
CONV_GROUP = 128
CONV_WIDTH = 3
CHUNK = 128
ROPE_THETA = 10000.0
EPS = 1e-6

V7X_VMEM_LIMIT_BYTES = 56 * 1024 * 1024


def _params(n_grid_axes):
    return pltpu.CompilerParams(
        dimension_semantics=("arbitrary",) * n_grid_axes,
        vmem_limit_bytes=V7X_VMEM_LIMIT_BYTES,
    )


def _rmsnorm_body(x_ref, g_ref, o_ref):
    x = x_ref[...]
    y = x * lax.rsqrt(jnp.mean(x * x, axis=-1, keepdims=True) + EPS)
    o_ref[...] = (y * g_ref[...]).astype(o_ref.dtype)


def _rmsnorm(x, gain, out_dtype, block_rows=512):
    t, d = x.shape
    return pl.pallas_call(
        _rmsnorm_body,
        grid=(t // block_rows,),
        in_specs=[
            pl.BlockSpec((block_rows, d), lambda i: (i, 0)),
            pl.BlockSpec((1, d), lambda i: (0, 0)),
        ],
        out_specs=pl.BlockSpec((block_rows, d), lambda i: (i, 0)),
        out_shape=jax.ShapeDtypeStruct((t, d), out_dtype),
        compiler_params=_params(1),
        name="rmsnorm",
    )(x, gain.reshape(1, d))


def _resident(block_shape, index_map):
    return pl.BlockSpec(block_shape, index_map, pipeline_mode=pl.Buffered(1))


def _dot(a, w_f32):
    return jnp.dot(a, w_f32.astype(BF16), preferred_element_type=F32)


def _gate_up_body(a_ref, wg_ref, wu_ref, o_ref):
    a = a_ref[...]
    gate = _dot(a, wg_ref[...])
    up = _dot(a, wu_ref[...])
    o_ref[...] = (jax.nn.silu(gate) * up).astype(o_ref.dtype)


def _gate_up(a, w_gate, w_up, layer, bm=2048, bn=256):
    t, d = a.shape
    ff = w_gate.shape[2]
    return pl.pallas_call(
        _gate_up_body,
        grid=(t // bm, ff // bn),
        in_specs=[
            _resident((bm, d), lambda i, j: (i, 0)),
            pl.BlockSpec((None, d, bn), lambda i, j: (layer, 0, j)),
            pl.BlockSpec((None, d, bn), lambda i, j: (layer, 0, j)),
        ],
        out_specs=pl.BlockSpec((bm, bn), lambda i, j: (i, j)),
        out_shape=jax.ShapeDtypeStruct((t, ff), BF16),
        compiler_params=_params(2),
        name="ffn_gate_up",
    )(a, w_gate, w_up)


def _down_body(a_ref, w_ref, r_ref, o_ref):
    o_ref[...] = r_ref[...] + 0.5 * _dot(a_ref[...], w_ref[...])


def _down(a, w, resid, layer, bm=1024, bn=256):
    t, ff = a.shape
    d = w.shape[2]
    return pl.pallas_call(
        _down_body,
        grid=(t // bm, d // bn),
        in_specs=[
            _resident((bm, ff), lambda i, j: (i, 0)),
            pl.BlockSpec((None, ff, bn), lambda i, j: (layer, 0, j)),
            pl.BlockSpec((bm, bn), lambda i, j: (i, j)),
        ],
        out_specs=pl.BlockSpec((bm, bn), lambda i, j: (i, j)),
        out_shape=jax.ShapeDtypeStruct((t, d), F32),
        compiler_params=_params(2),
        name="ffn_down",
    )(a, w, resid)


def _in_proj_body(a_ref, w_ref, o_ref):
    o_ref[...] = _dot(a_ref[...], w_ref[...])


def _in_proj(a, w, layer, bm=2048, bn=512):
    t, d = a.shape
    n = w.shape[2]
    return pl.pallas_call(
        _in_proj_body,
        grid=(t // bm, n // bn),
        in_specs=[
            _resident((bm, d), lambda i, j: (i, 0)),
            pl.BlockSpec((None, d, bn), lambda i, j: (layer, 0, j)),
        ],
        out_specs=pl.BlockSpec((bm, bn), lambda i, j: (i, j)),
        out_shape=jax.ShapeDtypeStruct((t, n), F32),
        compiler_params=_params(2),
        name="mixer_in_proj",
    )(a, w)


def _out_proj_body(a1_ref, a2_ref, w1_ref, w2_ref, r_ref, o_ref):
    acc = _dot(a1_ref[...], w1_ref[...]) + _dot(a2_ref[...], w2_ref[...])
    o_ref[...] = r_ref[...] + acc


def _out_proj(a_ret, a_conv, w, resid, layer, bm=2048, bn=512):
    t, d_ret = a_ret.shape
    d_conv = a_conv.shape[1]
    assert d_ret == d_conv
    n = w.shape[2]
    return pl.pallas_call(
        _out_proj_body,
        grid=(t // bm, n // bn),
        in_specs=[
            _resident((bm, d_ret), lambda i, j: (i, 0)),
            _resident((bm, d_conv), lambda i, j: (i, 0)),
            pl.BlockSpec((None, d_ret, bn), lambda i, j: (layer, 0, j)),
            pl.BlockSpec((None, d_conv, bn), lambda i, j: (layer, 1, j)),
            pl.BlockSpec((bm, bn), lambda i, j: (i, j)),
        ],
        out_specs=pl.BlockSpec((bm, bn), lambda i, j: (i, j)),
        out_shape=jax.ShapeDtypeStruct((t, n), F32),
        compiler_params=_params(2),
        name="mixer_out_proj",
    )(a_ret, a_conv, w, w, resid)


def _retention_body(q_ref, k_ref, v_ref, g_ref, cos_ref, sin_ref, lg_ref, gn_ref, o_ref):
    s, dh = q_ref.shape
    n_chunks = s // CHUNK
    lg = lg_ref[...]

    row = lax.broadcasted_iota(jnp.int32, (CHUNK, CHUNK), 0)
    col = lax.broadcasted_iota(jnp.int32, (CHUNK, CHUNK), 1)
    diff = (row - col).astype(F32)
    intra_decay = jnp.where(diff >= 0, jnp.exp(lg * jnp.maximum(diff, 0.0)), 0.0)
    idx = lax.broadcasted_iota(jnp.int32, (CHUNK, dh), 0).astype(F32)
    zeta = jnp.exp(lg * (CHUNK - 1.0 - idx))
    xi = jnp.exp(lg * (idx + 1.0))
    chunk_decay = jnp.exp(lg * float(CHUNK))

    def rope(x, cos, sin):
        return x * cos + pltpu.roll(x, dh // 2, 1) * sin

    state = jnp.zeros((dh, dh), F32)
    for c in range(n_chunks):
        rows = pl.ds(c * CHUNK, CHUNK)
        cos = cos_ref[rows, :]
        sin = sin_ref[rows, :]
        q = rope(q_ref[rows, :], cos, sin)
        k = rope(k_ref[rows, :], cos, sin) * (dh ** -0.5)
        v = v_ref[rows, :]
        q16 = q.astype(BF16)
        v16 = v.astype(BF16)

        scores = lax.dot_general(q16, k.astype(BF16), (((1,), (1,)), ((), ())),
                                 preferred_element_type=F32) * intra_decay
        out = jnp.dot(scores.astype(BF16), v16, preferred_element_type=F32)
        out = out + jnp.dot(q16, state.astype(BF16), preferred_element_type=F32) * xi
        kv = lax.dot_general((k * zeta).astype(BF16), v16, (((0,), (0,)), ((), ())),
                             preferred_element_type=F32)
        state = state * chunk_decay + kv

        out = out * lax.rsqrt(jnp.mean(out * out, axis=-1, keepdims=True) + EPS)
        out = out * gn_ref[...]
        o_ref[rows, :] = (jax.nn.silu(g_ref[rows, :]) * out).astype(o_ref.dtype)


def _rope_tables(seq, dh):
    half = dh // 2
    inv_freq = ROPE_THETA ** (-jnp.arange(half, dtype=F32) / half)
    ang = jnp.arange(seq, dtype=F32)[:, None] * inv_freq[None, :]
    cos = jnp.cos(ang)
    sin = jnp.sin(ang)
    return jnp.concatenate([cos, cos], axis=-1), jnp.concatenate([-sin, sin], axis=-1)


def _retention(proj, ret_norm):
    b, s, _ = proj.shape
    h, dh = RET_HEADS, HEAD_DIM
    cos, sin = _rope_tables(s, dh)
    log_gamma = jnp.log1p(-jnp.power(2.0, -5.0 - jnp.arange(h, dtype=F32)))
    lg = jnp.broadcast_to(log_gamma[:, None, None], (h, 1, dh))
    gn = ret_norm.astype(F32).reshape(h, 1, dh)

    def head_cols(group):
        return pl.BlockSpec((None, s, dh), lambda bi, hi: (bi, 0, group * h + hi))

    table = pl.BlockSpec((s, dh), lambda bi, hi: (0, 0))
    per_head = pl.BlockSpec((None, 1, dh), lambda bi, hi: (hi, 0, 0))
    return pl.pallas_call(
        _retention_body,
        grid=(b, h),
        in_specs=[head_cols(0), head_cols(1), head_cols(2), head_cols(3),
                  table, table, per_head, per_head],
        out_specs=pl.BlockSpec((None, s, dh), lambda bi, hi: (bi, 0, hi)),
        out_shape=jax.ShapeDtypeStruct((b, s, h * dh), BF16),
        compiler_params=_params(2),
        name="retention",
    )(proj, proj, proj, proj, cos, sin, lg, gn)


def _short_conv_body(b_ref, c_ref, u_ref, w_ref, o_ref):
    cu = c_ref[...] * u_ref[...]
    t = lax.broadcasted_iota(jnp.int32, cu.shape, 0)
    acc = w_ref[CONV_WIDTH - 1:CONV_WIDTH, :] * cu
    for lag in range(1, CONV_WIDTH):
        shifted = jnp.where(t >= lag, pltpu.roll(cu, lag, 0), 0.0)
        acc = acc + w_ref[CONV_WIDTH - 1 - lag:CONV_WIDTH - lag, :] * shifted
    o_ref[...] = (b_ref[...] * acc).astype(o_ref.dtype)


def _short_conv(proj, conv_w):
    b, s, _ = proj.shape
    d_conv = conv_w.shape[1]
    groups = d_conv // CONV_GROUP
    first = 4 * RET_HEADS * HEAD_DIM // CONV_GROUP

    def group_cols(which):
        return pl.BlockSpec((None, s, CONV_GROUP),
                            lambda bi, gi: (bi, 0, first + which * groups + gi))

    return pl.pallas_call(
        _short_conv_body,
        grid=(b, groups),
        in_specs=[group_cols(0), group_cols(1), group_cols(2),
                  pl.BlockSpec((CONV_WIDTH, CONV_GROUP), lambda bi, gi: (0, gi))],
        out_specs=pl.BlockSpec((None, s, CONV_GROUP), lambda bi, gi: (bi, 0, gi)),
        out_shape=jax.ShapeDtypeStruct((b, s, d_conv), BF16),
        compiler_params=_params(2),
        name="short_conv",
    )(proj, proj, proj, conv_w)


def _swiglu_residual(x, norm_gain, w_gate, w_up, w_down, layer):
    hidden = _gate_up(_rmsnorm(x, norm_gain, BF16), w_gate, w_up, layer)
    return _down(hidden, w_down, x, layer)


def kernel(x, norm_ffa, w_ffa_gate, w_ffa_up, w_ffa_down, norm_mix, w_in, conv_w, ret_norm,
           w_out, norm_ffb, w_ffb_gate, w_ffb_up, w_ffb_down, norm_final):
    batch, seq, d_model = x.shape
    depth = norm_ffa.shape[0]
    x = x.reshape(batch * seq, d_model)
    for l in range(depth):
        x = _swiglu_residual(x, norm_ffa[l], w_ffa_gate, w_ffa_up, w_ffa_down, l)
        proj = _in_proj(_rmsnorm(x, norm_mix[l], BF16), w_in, l).reshape(batch, seq, -1)
        ret_out = _retention(proj, ret_norm[l]).reshape(batch * seq, -1)
        conv_out = _short_conv(proj, conv_w[l]).reshape(batch * seq, -1)
        x = _out_proj(ret_out, conv_out, w_out, x, l)
        x = _swiglu_residual(x, norm_ffb[l], w_ffb_gate, w_ffb_up, w_ffb_down, l)
    return _rmsnorm(x, norm_final, F32).reshape(batch, seq, d_model)
```

```python
import functools

import jax
import jax.numpy as jnp
from jax import lax
from jax.experimental import pallas as pl
from jax.experimental.pallas import tpu as pltpu

F32 = jnp.float32
BF16 = jnp.bfloat16

RET_HEADS = 16
HEAD_DIM = 128
CONV_GROUP = 128
CONV_WIDTH = 3
CHUNK = 128
ROPE_THETA = 10000.0
EPS = 1e-6

LANES = 128
V7X_VMEM_LIMIT_BYTES = 60 * 1024 * 1024


def _params(n_grid_axes):
    return pltpu.CompilerParams(
        dimension_semantics=("arbitrary",) * n_grid_axes,
        vmem_limit_bytes=V7X_VMEM_LIMIT_BYTES,
    )


def _resident(block_shape, index_map):
    return pl.BlockSpec(block_shape, index_map, pipeline_mode=pl.Buffered(1))


def _tiles(total, tile):
    return [(o, tile) for o in range(0, total, tile)]


def _lane_tile(r, width):
    return r if width == LANES else jnp.concatenate([r] * (width // LANES), axis=1)


def _lane_block_sum(x):
    acc = x[:, :LANES]
    for c in range(LANES, x.shape[1], LANES):
        acc = acc + x[:, c:c + LANES]
    return acc


def _rmsnorm_body(x_ref, g_ref, o_ref):
    x = x_ref[...]
    y = x * lax.rsqrt(jnp.mean(x * x, axis=-1, keepdims=True) + EPS)
    o_ref[...] = (y * g_ref[...]).astype(o_ref.dtype)


def _rmsnorm(x, gain, block_rows=512):
    t, d = x.shape
    return pl.pallas_call(
        _rmsnorm_body,
        grid=(t // block_rows,),
        in_specs=[
            pl.BlockSpec((block_rows, d), lambda i: (i, 0)),
            pl.BlockSpec((1, d), lambda i: (0, 0)),
        ],
        out_specs=pl.BlockSpec((block_rows, d), lambda i: (i, 0)),
        out_shape=jax.ShapeDtypeStruct((t, d), F32),
        compiler_params=_params(1),
        name="rmsnorm",
    )(x, gain.reshape(1, d))


def _norm_split_body(x_ref, g_ref, xg_ref, r_ref):
    x = x_ref[...]
    xg_ref[...] = (x * g_ref[...]).astype(xg_ref.dtype)
    r = lax.rsqrt(jnp.mean(x * x, axis=-1, keepdims=True) + EPS)
    r_ref[...] = jnp.broadcast_to(r, r_ref.shape)


def _norm_split(x, gain, block_rows=512):
    t, d = x.shape
    return pl.pallas_call(
        _norm_split_body,
        grid=(t // block_rows,),
        in_specs=[
            pl.BlockSpec((block_rows, d), lambda i: (i, 0)),
            pl.BlockSpec((1, d), lambda i: (0, 0)),
        ],
        out_specs=[
            pl.BlockSpec((block_rows, d), lambda i: (i, 0)),
            pl.BlockSpec((block_rows, LANES), lambda i: (i, 0)),
        ],
        out_shape=[jax.ShapeDtypeStruct((t, d), BF16), jax.ShapeDtypeStruct((t, LANES), F32)],
        compiler_params=_params(1),
        name="norm_split",
    )(x, gain.reshape(1, d))


def _ssq_start(ssq_ref):
    @pl.when(pl.program_id(1) == 0)
    def _():
        ssq_ref[...] = jnp.zeros(ssq_ref.shape, ssq_ref.dtype)


def _emit_scaled_and_ssq(x_new, rows, cols, g_ref, xg_ref, ssq_ref):
    r0, rt = rows
    c0, ct = cols
    xg_ref[r0:r0 + rt, c0:c0 + ct] = (x_new * g_ref[:, c0:c0 + ct]).astype(xg_ref.dtype)
    ssq_ref[r0:r0 + rt, :] += _lane_block_sum(x_new * x_new)


def _ssq_finish(ssq_ref, r_ref, d_model):
    @pl.when(pl.program_id(1) == pl.num_programs(1) - 1)
    def _():
        total = jnp.sum(ssq_ref[...], axis=-1, keepdims=True)
        r = lax.rsqrt(total * (1.0 / d_model) + EPS)
        r_ref[...] = jnp.broadcast_to(r, r_ref.shape)


def _gate_up_body(a_ref, r_ref, wg_ref, wu_ref, o_ref, *, row_tile, col_tile):
    bm, bn = o_ref.shape
    wg = wg_ref[...].astype(BF16)
    wu = wu_ref[...].astype(BF16)
    for r0, rt in _tiles(bm, row_tile):
        a = a_ref[r0:r0 + rt, :]
        for c0, ct in _tiles(bn, col_tile):
            scale = _lane_tile(r_ref[r0:r0 + rt, :], ct)
            gate = jnp.dot(a, wg[:, c0:c0 + ct], preferred_element_type=F32) * scale
            up = jnp.dot(a, wu[:, c0:c0 + ct], preferred_element_type=F32) * scale
            o_ref[r0:r0 + rt, c0:c0 + ct] = (jax.nn.silu(gate) * up).astype(o_ref.dtype)


def _gate_up(xg, r, w_gate, w_up, layer, bm=2048, bn=256, row_tile=512, col_tile=256, tag=""):
    t, d = xg.shape
    ff = w_gate.shape[2]
    body = functools.partial(_gate_up_body, row_tile=row_tile, col_tile=col_tile)
    return pl.pallas_call(
        body,
        grid=(t // bm, ff // bn),
        in_specs=[
            _resident((bm, d), lambda i, j: (i, 0)),
            _resident((bm, LANES), lambda i, j: (i, 0)),
            pl.BlockSpec((None, d, bn), lambda i, j: (layer, 0, j)),
            pl.BlockSpec((None, d, bn), lambda i, j: (layer, 0, j)),
        ],
        out_specs=pl.BlockSpec((bm, bn), lambda i, j: (i, j)),
        out_shape=jax.ShapeDtypeStruct((t, ff), BF16),
        compiler_params=_params(2),
        name="gate_up" + tag,
    )(xg, r, w_gate, w_up)


def _down_body(a_ref, w_ref, res_ref, *rest, d_model, emit_norm):
    if emit_norm:
        g_ref, o_ref, xg_ref, r_ref, ssq_ref = rest
        _ssq_start(ssq_ref)
    else:
        (o_ref,) = rest
    x_new = res_ref[...] + 0.5 * jnp.dot(a_ref[...], w_ref[...].astype(BF16),
                                         preferred_element_type=F32)
    o_ref[...] = x_new
    if emit_norm:
        bm, bn = o_ref.shape
        _emit_scaled_and_ssq(x_new, (0, bm), (0, bn), g_ref, xg_ref, ssq_ref)
        _ssq_finish(ssq_ref, r_ref, d_model)


def _down(a, w, resid, layer, next_gain=None, bm=1024, bn=256):
    t, ff = a.shape
    d = w.shape[2]
    emit_norm = next_gain is not None
    in_specs = [
        _resident((bm, ff), lambda i, j: (i, 0)),
        pl.BlockSpec((None, ff, bn), lambda i, j: (layer, 0, j)),
        pl.BlockSpec((bm, bn), lambda i, j: (i, j)),
    ]
    out_specs = [pl.BlockSpec((bm, bn), lambda i, j: (i, j))]
    out_shape = [jax.ShapeDtypeStruct((t, d), F32)]
    args = [a, w, resid]
    scratch = []
    if emit_norm:
        in_specs.append(pl.BlockSpec((1, bn), lambda i, j: (0, j)))
        out_specs += [pl.BlockSpec((bm, bn), lambda i, j: (i, j)),
                      pl.BlockSpec((bm, LANES), lambda i, j: (i, 0))]
        out_shape += [jax.ShapeDtypeStruct((t, d), BF16), jax.ShapeDtypeStruct((t, LANES), F32)]
        args.append(next_gain.reshape(1, d))
        scratch = [pltpu.VMEM((bm, LANES), F32)]
    out = pl.pallas_call(
        functools.partial(_down_body, d_model=d, emit_norm=emit_norm),
        grid=(t // bm, d // bn),
        in_specs=in_specs,
        out_specs=out_specs,
        out_shape=out_shape,
        scratch_shapes=scratch,
        compiler_params=_params(2),
        name="down",
    )(*args)
    return out if emit_norm else out[0]


def _in_proj_body(a_ref, r_ref, w_ref, o_ref, *, row_tile, col_tile):
    bm, bn = o_ref.shape
    w = w_ref[...].astype(BF16)
    for r0, rt in _tiles(bm, row_tile):
        a = a_ref[r0:r0 + rt, :]
        for c0, ct in _tiles(bn, col_tile):
            scale = _lane_tile(r_ref[r0:r0 + rt, :], ct)
            o_ref[r0:r0 + rt, c0:c0 + ct] = (
                jnp.dot(a, w[:, c0:c0 + ct], preferred_element_type=F32) * scale)


def _in_proj(xg, r, w, layer, bm=2048, bn=512, row_tile=1024, col_tile=256, tag=""):
    t, d = xg.shape
    n = w.shape[2]
    body = functools.partial(_in_proj_body, row_tile=row_tile, col_tile=col_tile)
    return pl.pallas_call(
        body,
        grid=(t // bm, n // bn),
        in_specs=[
            _resident((bm, d), lambda i, j: (i, 0)),
            _resident((bm, LANES), lambda i, j: (i, 0)),
            pl.BlockSpec((None, d, bn), lambda i, j: (layer, 0, j)),
        ],
        out_specs=pl.BlockSpec((bm, bn), lambda i, j: (i, j)),
        out_shape=jax.ShapeDtypeStruct((t, n), F32),
        compiler_params=_params(2),
        name="in_proj" + tag,
    )(xg, r, w)


def _out_proj_body(a1_ref, a2_ref, w1_ref, w2_ref, res_ref, g_ref, o_ref, xg_ref, r_ref, ssq_ref,
                   *, row_tile, col_tile, d_model):
    bm, bn = o_ref.shape
    _ssq_start(ssq_ref)
    w1 = w1_ref[...].astype(BF16)
    w2 = w2_ref[...].astype(BF16)
    for r0, rt in _tiles(bm, row_tile):
        a1 = a1_ref[r0:r0 + rt, :]
        a2 = a2_ref[r0:r0 + rt, :]
        for c0, ct in _tiles(bn, col_tile):
            acc = (jnp.dot(a1, w1[:, c0:c0 + ct], preferred_element_type=F32)
                   + jnp.dot(a2, w2[:, c0:c0 + ct], preferred_element_type=F32))
            x_new = res_ref[r0:r0 + rt, c0:c0 + ct] + acc
            o_ref[r0:r0 + rt, c0:c0 + ct] = x_new
            _emit_scaled_and_ssq(x_new, (r0, rt), (c0, ct), g_ref, xg_ref, ssq_ref)
    _ssq_finish(ssq_ref, r_ref, d_model)


def _out_proj(a_ret, a_conv, w, resid, layer, next_gain, bm=2048, bn=512, row_tile=1024,
              col_tile=256, tag=""):
    t, d_ret = a_ret.shape
    d_conv = a_conv.shape[1]
    assert d_ret == d_conv
    n = w.shape[2]
    body = functools.partial(_out_proj_body, row_tile=row_tile, col_tile=col_tile, d_model=n)
    return pl.pallas_call(
        body,
        grid=(t // bm, n // bn),
        in_specs=[
            _resident((bm, d_ret), lambda i, j: (i, 0)),
            _resident((bm, d_conv), lambda i, j: (i, 0)),
            pl.BlockSpec((None, d_ret, bn), lambda i, j: (layer, 0, j)),
            pl.BlockSpec((None, d_conv, bn), lambda i, j: (layer, 1, j)),
            pl.BlockSpec((bm, bn), lambda i, j: (i, j)),
            pl.BlockSpec((1, bn), lambda i, j: (0, j)),
        ],
        out_specs=[
            pl.BlockSpec((bm, bn), lambda i, j: (i, j)),
            pl.BlockSpec((bm, bn), lambda i, j: (i, j)),
            pl.BlockSpec((bm, LANES), lambda i, j: (i, 0)),
        ],
        out_shape=[jax.ShapeDtypeStruct((t, n), F32), jax.ShapeDtypeStruct((t, n), BF16),
                   jax.ShapeDtypeStruct((t, LANES), F32)],
        scratch_shapes=[pltpu.VMEM((bm, LANES), F32)],
        compiler_params=_params(2),
        name="out_proj" + tag,
    )(a_ret, a_conv, w, w, resid, next_gain.reshape(1, n))


def _retention_body(q_ref, k_ref, v_ref, g_ref, cos_ref, sin_ref, lg_ref, gn_ref, o_ref):
    s, dh = q_ref.shape
    n_chunks = s // CHUNK
    lg = lg_ref[...]

    row = lax.broadcasted_iota(jnp.int32, (CHUNK, CHUNK), 0)
    col = lax.broadcasted_iota(jnp.int32, (CHUNK, CHUNK), 1)
    diff = (row - col).astype(F32)
    intra_decay = jnp.where(diff >= 0, jnp.exp(lg * jnp.maximum(diff, 0.0)), 0.0)
    idx = lax.broadcasted_iota(jnp.int32, (CHUNK, dh), 0).astype(F32)
    zeta = jnp.exp(lg * (CHUNK - 1.0 - idx))
    xi = jnp.exp(lg * (idx + 1.0))
    chunk_decay = jnp.exp(lg * float(CHUNK))

    def rope(x, cos, sin):
        return x * cos + pltpu.roll(x, dh // 2, 1) * sin

    state = jnp.zeros((dh, dh), F32)
    for c in range(n_chunks):
        rows = pl.ds(c * CHUNK, CHUNK)
        cos = cos_ref[rows, :]
        sin = sin_ref[rows, :]
        q = rope(q_ref[rows, :], cos, sin)
        k = rope(k_ref[rows, :], cos, sin) * (dh ** -0.5)
        v = v_ref[rows, :]
        q16 = q.astype(BF16)
        v16 = v.astype(BF16)

        scores = lax.dot_general(q16, k.astype(BF16), (((1,), (1,)), ((), ())),
                                 preferred_element_type=F32) * intra_decay
        out = jnp.dot(scores.astype(BF16), v16, preferred_element_type=F32)
        out = out + jnp.dot(q16, state.astype(BF16), preferred_element_type=F32) * xi
        kv = lax.dot_general((k * zeta).astype(BF16), v16, (((0,), (0,)), ((), ())),
                             preferred_element_type=F32)
        state = state * chunk_decay + kv

        out = out * lax.rsqrt(jnp.mean(out * out, axis=-1, keepdims=True) + EPS)
        out = out * gn_ref[...]
        o_ref[rows, :] = (jax.nn.silu(g_ref[rows, :]) * out).astype(o_ref.dtype)


def _rope_tables(seq, dh):
    half = dh // 2
    inv_freq = ROPE_THETA ** (-jnp.arange(half, dtype=F32) / half)
    ang = jnp.arange(seq, dtype=F32)[:, None] * inv_freq[None, :]
    cos = jnp.cos(ang)
    sin = jnp.sin(ang)
    return jnp.concatenate([cos, cos], axis=-1), jnp.concatenate([-sin, sin], axis=-1)


def _retention(proj, ret_norm):
    b, s, _ = proj.shape
    h, dh = RET_HEADS, HEAD_DIM
    cos, sin = _rope_tables(s, dh)
    log_gamma = jnp.log1p(-jnp.power(2.0, -5.0 - jnp.arange(h, dtype=F32)))
    lg = jnp.broadcast_to(log_gamma[:, None, None], (h, 1, dh))
    gn = ret_norm.astype(F32).reshape(h, 1, dh)

    def head_cols(group):
        return pl.BlockSpec((None, s, dh), lambda bi, hi: (bi, 0, group * h + hi))

    table = pl.BlockSpec((s, dh), lambda bi, hi: (0, 0))
    per_head = pl.BlockSpec((None, 1, dh), lambda bi, hi: (hi, 0, 0))
    return pl.pallas_call(
        _retention_body,
        grid=(b, h),
        in_specs=[head_cols(0), head_cols(1), head_cols(2), head_cols(3),
                  table, table, per_head, per_head],
        out_specs=pl.BlockSpec((None, s, dh), lambda bi, hi: (bi, 0, hi)),
        out_shape=jax.ShapeDtypeStruct((b, s, h * dh), BF16),
        compiler_params=_params(2),
        name="retention",
    )(proj, proj, proj, proj, cos, sin, lg, gn)


def _short_conv_body(b_ref, c_ref, u_ref, w_ref, o_ref):
    cu = c_ref[...] * u_ref[...]
    t = lax.broadcasted_iota(jnp.int32, cu.shape, 0)
    acc = w_ref[CONV_WIDTH - 1:CONV_WIDTH, :] * cu
    for lag in range(1, CONV_WIDTH):
        shifted = jnp.where(t >= lag, pltpu.roll(cu, lag, 0), 0.0)
        acc = acc + w_ref[CONV_WIDTH - 1 - lag:CONV_WIDTH - lag, :] * shifted
    o_ref[...] = (b_ref[...] * acc).astype(o_ref.dtype)


def _short_conv(proj, conv_w):
    b, s, _ = proj.shape
    d_conv = conv_w.shape[1]
    groups = d_conv // CONV_GROUP
    first = 4 * RET_HEADS * HEAD_DIM // CONV_GROUP

    def group_cols(which):
        return pl.BlockSpec((None, s, CONV_GROUP),
                            lambda bi, gi: (bi, 0, first + which * groups + gi))

    return pl.pallas_call(
        _short_conv_body,
        grid=(b, groups),
        in_specs=[group_cols(0), group_cols(1), group_cols(2),
                  pl.BlockSpec((CONV_WIDTH, CONV_GROUP), lambda bi, gi: (0, gi))],
        out_specs=pl.BlockSpec((None, s, CONV_GROUP), lambda bi, gi: (bi, 0, gi)),
        out_shape=jax.ShapeDtypeStruct((b, s, d_conv), BF16),
        compiler_params=_params(2),
        name="short_conv",
    )(proj, proj, proj, conv_w)


_GATE_UP_CFG = [dict(row_tile=512, tag="_rt512a"), dict(row_tile=256, tag="_rt256a"),
                dict(row_tile=512, tag="_rt512b"), dict(row_tile=256, tag="_rt256b")]
_IN_PROJ_CFG = [dict(row_tile=1024, tag="_rt1024"), dict(row_tile=512, tag="_rt512")]
_OUT_PROJ_CFG = [dict(row_tile=1024, tag="_rt1024"), dict(row_tile=512, tag="_rt512")]


def kernel(x, norm_ffa, w_ffa_gate, w_ffa_up, w_ffa_down, norm_mix, w_in, conv_w, ret_norm,
           w_out, norm_ffb, w_ffb_gate, w_ffb_up, w_ffb_down, norm_final):
    batch, seq, d_model = x.shape
    depth = norm_ffa.shape[0]
    x = x.reshape(batch * seq, d_model)
    xg, r = _norm_split(x, norm_ffa[0])
    for l in range(depth):
        hidden = _gate_up(xg, r, w_ffa_gate, w_ffa_up, l, **_GATE_UP_CFG[2 * l])
        x, xg, r = _down(hidden, w_ffa_down, x, l, next_gain=norm_mix[l])
        proj = _in_proj(xg, r, w_in, l, **_IN_PROJ_CFG[l]).reshape(batch, seq, -1)
        ret_out = _retention(proj, ret_norm[l]).reshape(batch * seq, -1)
        conv_out = _short_conv(proj, conv_w[l]).reshape(batch * seq, -1)
        x, xg, r = _out_proj(ret_out, conv_out, w_out, x, l, norm_ffb[l], **_OUT_PROJ_CFG[l])
        hidden = _gate_up(xg, r, w_ffb_gate, w_ffb_up, l, **_GATE_UP_CFG[2 * l + 1])
        if l + 1 < depth:
            x, xg, r = _down(hidden, w_ffb_down, x, l, next_gain=norm_ffa[l + 1])
        else:
            x = _down(hidden, w_ffb_down, x, l)
    return _rmsnorm(x, norm_final).reshape(batch, seq, d_model)
```

```python
import functools

import jax
import jax.numpy as jnp
from jax import lax
from jax.experimental import pallas as pl
from jax.experimental.pallas import tpu as pltpu

F32 = jnp.float32
BF16 = jnp.bfloat16

RET_HEADS = 16
HEAD_DIM = 128
CONV_GROUP = 128
CONV_WIDTH = 3
CHUNK = 128
ROPE_THETA = 10000.0
EPS = 1e-6

LANES = 128
V7X_VMEM_LIMIT_BYTES = 60 * 1024 * 1024


def _params(n_grid_axes):
    return pltpu.CompilerParams(
        dimension_semantics=("arbitrary",) * n_grid_axes,
        vmem_limit_bytes=V7X_VMEM_LIMIT_BYTES,
    )


def _resident(block_shape, index_map):
    return pl.BlockSpec(block_shape, index_map, pipeline_mode=pl.Buffered(1))


def _tiles(total, tile):
    return [(o, tile) for o in range(0, total, tile)]


def _lane_tile(r, width):
    return r if width == LANES else jnp.concatenate([r] * (width // LANES), axis=1)


def _lane_block_sum(x):
    acc = x[:, :LANES]
    for c in range(LANES, x.shape[1], LANES):
        acc = acc + x[:, c:c + LANES]
    return acc


def _rmsnorm_body(x_ref, g_ref, o_ref):
    x = x_ref[...]
    y = x * lax.rsqrt(jnp.mean(x * x, axis=-1, keepdims=True) + EPS)
    o_ref[...] = (y * g_ref[...]).astype(o_ref.dtype)


def _rmsnorm(x, gain, block_rows=512):
    t, d = x.shape
    return pl.pallas_call(
        _rmsnorm_body,
        grid=(t // block_rows,),
        in_specs=[
            pl.BlockSpec((block_rows, d), lambda i: (i, 0)),
            pl.BlockSpec((1, d), lambda i: (0, 0)),
        ],
        out_specs=pl.BlockSpec((block_rows, d), lambda i: (i, 0)),
        out_shape=jax.ShapeDtypeStruct((t, d), F32),
        compiler_params=_params(1),
        name="rmsnorm",
    )(x, gain.reshape(1, d))


def _norm_split_body(x_ref, g_ref, xg_ref, r_ref):
    x = x_ref[...]
    xg_ref[...] = (x * g_ref[...]).astype(xg_ref.dtype)
    r = lax.rsqrt(jnp.mean(x * x, axis=-1, keepdims=True) + EPS)
    r_ref[...] = jnp.broadcast_to(r, r_ref.shape)


def _norm_split(x, gain, block_rows=512):
    t, d = x.shape
    return pl.pallas_call(
        _norm_split_body,
        grid=(t // block_rows,),
        in_specs=[
            pl.BlockSpec((block_rows, d), lambda i: (i, 0)),
            pl.BlockSpec((1, d), lambda i: (0, 0)),
        ],
        out_specs=[
            pl.BlockSpec((block_rows, d), lambda i: (i, 0)),
            pl.BlockSpec((block_rows, LANES), lambda i: (i, 0)),
        ],
        out_shape=[jax.ShapeDtypeStruct((t, d), BF16), jax.ShapeDtypeStruct((t, LANES), F32)],
        compiler_params=_params(1),
        name="norm_split",
    )(x, gain.reshape(1, d))


def _ssq_start(ssq_ref):
    @pl.when(pl.program_id(1) == 0)
    def _():
        ssq_ref[...] = jnp.zeros(ssq_ref.shape, ssq_ref.dtype)


def _emit_scaled_and_ssq(x_new, rows, cols, g_ref, xg_ref, ssq_ref):
    r0, rt = rows
    c0, ct = cols
    xg_ref[r0:r0 + rt, c0:c0 + ct] = (x_new * g_ref[:, c0:c0 + ct]).astype(xg_ref.dtype)
    ssq_ref[r0:r0 + rt, :] += _lane_block_sum(x_new * x_new)


def _ssq_finish(ssq_ref, r_ref, d_model):
    @pl.when(pl.program_id(1) == pl.num_programs(1) - 1)
    def _():
        total = jnp.sum(ssq_ref[...], axis=-1, keepdims=True)
        r = lax.rsqrt(total * (1.0 / d_model) + EPS)
        r_ref[...] = jnp.broadcast_to(r, r_ref.shape)


def _gate_up_body(a_ref, r_ref, wg_ref, wu_ref, wd_ref, o_ref, wd16_ref, *, row_tile, col_tile):
    bm, bn = o_ref.shape

    @pl.when(pl.program_id(0) == 0)
    def _():
        wd16_ref[...] = wd_ref[...].astype(wd16_ref.dtype)

    wg = wg_ref[...].astype(BF16)
    wu = wu_ref[...].astype(BF16)
    for r0, rt in _tiles(bm, row_tile):
        a = a_ref[r0:r0 + rt, :]
        for c0, ct in _tiles(bn, col_tile):
            scale = _lane_tile(r_ref[r0:r0 + rt, :], ct)
            gate = jnp.dot(a, wg[:, c0:c0 + ct], preferred_element_type=F32) * scale
            up = jnp.dot(a, wu[:, c0:c0 + ct], preferred_element_type=F32) * scale
            o_ref[r0:r0 + rt, c0:c0 + ct] = (jax.nn.silu(gate) * up).astype(o_ref.dtype)


def _gate_up(xg, r, w_gate, w_up, w_down, layer, bm=2048, bn=256, row_tile=512, col_tile=256):
    t, d = xg.shape
    ff = w_gate.shape[2]
    n_j = ff // bn

    def wd_block(i, j):
        return jnp.where(i == 0, j, n_j - 1)

    body = functools.partial(_gate_up_body, row_tile=row_tile, col_tile=col_tile)
    return pl.pallas_call(
        body,
        grid=(t // bm, n_j),
        in_specs=[
            _resident((bm, d), lambda i, j: (i, 0)),
            _resident((bm, LANES), lambda i, j: (i, 0)),
            pl.BlockSpec((None, d, bn), lambda i, j: (layer, 0, j)),
            pl.BlockSpec((None, d, bn), lambda i, j: (layer, 0, j)),
            pl.BlockSpec((None, bn, d), lambda i, j: (layer, wd_block(i, j), 0)),
        ],
        out_specs=[
            pl.BlockSpec((bm, bn), lambda i, j: (i, j)),
            pl.BlockSpec((bn, d), lambda i, j: (wd_block(i, j), 0)),
        ],
        out_shape=[jax.ShapeDtypeStruct((t, ff), BF16), jax.ShapeDtypeStruct((ff, d), BF16)],
        compiler_params=_params(2),
        name="gate_up",
    )(xg, r, w_gate, w_up, w_down)


def _down_body(a_ref, w_ref, res_ref, *rest, d_model, emit_norm, col_tile):
    if emit_norm:
        g_ref, o_ref, xg_ref, r_ref, ssq_ref = rest
        _ssq_start(ssq_ref)
    else:
        (o_ref,) = rest
    bm, bn = o_ref.shape
    a = a_ref[...]
    for c0, ct in _tiles(bn, col_tile):
        x_new = res_ref[:, c0:c0 + ct] + 0.5 * jnp.dot(a, w_ref[:, c0:c0 + ct],
                                                      preferred_element_type=F32)
        o_ref[:, c0:c0 + ct] = x_new
        if emit_norm:
            _emit_scaled_and_ssq(x_new, (0, bm), (c0, ct), g_ref, xg_ref, ssq_ref)
    if emit_norm:
        _ssq_finish(ssq_ref, r_ref, d_model)


def _down(a, w16, resid, next_gain=None, bm=1024, bn=512, col_tile=256):
    t, ff = a.shape
    d = w16.shape[1]
    emit_norm = next_gain is not None
    in_specs = [
        _resident((bm, ff), lambda i, j: (i, 0)),
        pl.BlockSpec((ff, bn), lambda i, j: (0, j)),
        pl.BlockSpec((bm, bn), lambda i, j: (i, j)),
    ]
    out_specs = [pl.BlockSpec((bm, bn), lambda i, j: (i, j))]
    out_shape = [jax.ShapeDtypeStruct((t, d), F32)]
    args = [a, w16, resid]
    scratch = []
    if emit_norm:
        in_specs.append(pl.BlockSpec((1, bn), lambda i, j: (0, j)))
        out_specs += [pl.BlockSpec((bm, bn), lambda i, j: (i, j)),
                      pl.BlockSpec((bm, LANES), lambda i, j: (i, 0))]
        out_shape += [jax.ShapeDtypeStruct((t, d), BF16), jax.ShapeDtypeStruct((t, LANES), F32)]
        args.append(next_gain.reshape(1, d))
        scratch = [pltpu.VMEM((bm, LANES), F32)]
    out = pl.pallas_call(
        functools.partial(_down_body, d_model=d, emit_norm=emit_norm, col_tile=col_tile),
        grid=(t // bm, d // bn),
        in_specs=in_specs,
        out_specs=out_specs,
        out_shape=out_shape,
        scratch_shapes=scratch,
        compiler_params=_params(2),
        name="down",
    )(*args)
    return out if emit_norm else out[0]


def _in_proj_body(a_ref, r_ref, w_ref, o_ref, *, row_tile, col_tile):
    bm, bn = o_ref.shape
    w = w_ref[...].astype(BF16)
    for r0, rt in _tiles(bm, row_tile):
        a = a_ref[r0:r0 + rt, :]
        for c0, ct in _tiles(bn, col_tile):
            scale = _lane_tile(r_ref[r0:r0 + rt, :], ct)
            o_ref[r0:r0 + rt, c0:c0 + ct] = (
                jnp.dot(a, w[:, c0:c0 + ct], preferred_element_type=F32) * scale)


def _in_proj(xg, r, w, layer, bm=2048, bn=512, row_tile=512, col_tile=256):
    t, d = xg.shape
    n = w.shape[2]
    body = functools.partial(_in_proj_body, row_tile=row_tile, col_tile=col_tile)
    return pl.pallas_call(
        body,
        grid=(t // bm, n // bn),
        in_specs=[
            _resident((bm, d), lambda i, j: (i, 0)),
            _resident((bm, LANES), lambda i, j: (i, 0)),
            pl.BlockSpec((None, d, bn), lambda i, j: (layer, 0, j)),
        ],
        out_specs=pl.BlockSpec((bm, bn), lambda i, j: (i, j)),
        out_shape=jax.ShapeDtypeStruct((t, n), F32),
        compiler_params=_params(2),
        name="in_proj",
    )(xg, r, w)


def _out_proj_body(a1_ref, a2_ref, w1_ref, w2_ref, res_ref, g_ref, o_ref, xg_ref, r_ref, ssq_ref,
                   *, row_tile, col_tile, d_model):
    bm, bn = o_ref.shape
    _ssq_start(ssq_ref)
    w1 = w1_ref[...].astype(BF16)
    w2 = w2_ref[...].astype(BF16)
    for r0, rt in _tiles(bm, row_tile):
        a1 = a1_ref[r0:r0 + rt, :]
        a2 = a2_ref[r0:r0 + rt, :]
        for c0, ct in _tiles(bn, col_tile):
            acc = (jnp.dot(a1, w1[:, c0:c0 + ct], preferred_element_type=F32)
                   + jnp.dot(a2, w2[:, c0:c0 + ct], preferred_element_type=F32))
            x_new = res_ref[r0:r0 + rt, c0:c0 + ct] + acc
            o_ref[r0:r0 + rt, c0:c0 + ct] = x_new
            _emit_scaled_and_ssq(x_new, (r0, rt), (c0, ct), g_ref, xg_ref, ssq_ref)
    _ssq_finish(ssq_ref, r_ref, d_model)


def _out_proj(a_ret, a_conv, w, resid, layer, next_gain, bm=2048, bn=512, row_tile=1024,
              col_tile=256):
    t, d_ret = a_ret.shape
    d_conv = a_conv.shape[1]
    assert d_ret == d_conv
    n = w.shape[2]
    body = functools.partial(_out_proj_body, row_tile=row_tile, col_tile=col_tile, d_model=n)
    return pl.pallas_call(
        body,
        grid=(t // bm, n // bn),
        in_specs=[
            _resident((bm, d_ret), lambda i, j: (i, 0)),
            _resident((bm, d_conv), lambda i, j: (i, 0)),
            pl.BlockSpec((None, d_ret, bn), lambda i, j: (layer, 0, j)),
            pl.BlockSpec((None, d_conv, bn), lambda i, j: (layer, 1, j)),
            pl.BlockSpec((bm, bn), lambda i, j: (i, j)),
            pl.BlockSpec((1, bn), lambda i, j: (0, j)),
        ],
        out_specs=[
            pl.BlockSpec((bm, bn), lambda i, j: (i, j)),
            pl.BlockSpec((bm, bn), lambda i, j: (i, j)),
            pl.BlockSpec((bm, LANES), lambda i, j: (i, 0)),
        ],
        out_shape=[jax.ShapeDtypeStruct((t, n), F32), jax.ShapeDtypeStruct((t, n), BF16),
                   jax.ShapeDtypeStruct((t, LANES), F32)],
        scratch_shapes=[pltpu.VMEM((bm, LANES), F32)],
        compiler_params=_params(2),
        name="out_proj",
    )(a_ret, a_conv, w, w, resid, next_gain.reshape(1, n))


def _retention_body(q_ref, k_ref, v_ref, g_ref, cos_ref, sin_ref, lg_ref, gn_ref, o_ref):
    s, dh = q_ref.shape
    n_chunks = s // CHUNK
    lg = lg_ref[...]

    row = lax.broadcasted_iota(jnp.int32, (CHUNK, CHUNK), 0)
    col = lax.broadcasted_iota(jnp.int32, (CHUNK, CHUNK), 1)
    diff = (row - col).astype(F32)
    intra_decay = jnp.where(diff >= 0, jnp.exp(lg * jnp.maximum(diff, 0.0)), 0.0)
    idx = lax.broadcasted_iota(jnp.int32, (CHUNK, dh), 0).astype(F32)
    zeta = jnp.exp(lg * (CHUNK - 1.0 - idx))
    xi = jnp.exp(lg * (idx + 1.0))
    chunk_decay = jnp.exp(lg * float(CHUNK))

    def rope(x, cos, sin):
        return x * cos + pltpu.roll(x, dh // 2, 1) * sin

    state = jnp.zeros((dh, dh), F32)
    for c in range(n_chunks):
        rows = pl.ds(c * CHUNK, CHUNK)
        cos = cos_ref[rows, :]
        sin = sin_ref[rows, :]
        q = rope(q_ref[rows, :], cos, sin)
        k = rope(k_ref[rows, :], cos, sin) * (dh ** -0.5)
        v = v_ref[rows, :]
        q16 = q.astype(BF16)
        v16 = v.astype(BF16)

        scores = lax.dot_general(q16, k.astype(BF16), (((1,), (1,)), ((), ())),
                                 preferred_element_type=F32) * intra_decay
        out = jnp.dot(scores.astype(BF16), v16, preferred_element_type=F32)
        out = out + jnp.dot(q16, state.astype(BF16), preferred_element_type=F32) * xi
        kv = lax.dot_general((k * zeta).astype(BF16), v16, (((0,), (0,)), ((), ())),
                             preferred_element_type=F32)
        state = state * chunk_decay + kv

        out = out * lax.rsqrt(jnp.mean(out * out, axis=-1, keepdims=True) + EPS)
        out = out * gn_ref[...]
        o_ref[rows, :] = (jax.nn.silu(g_ref[rows, :]) * out).astype(o_ref.dtype)


def _rope_tables(seq, dh):
    half = dh // 2
    inv_freq = ROPE_THETA ** (-jnp.arange(half, dtype=F32) / half)
    ang = jnp.arange(seq, dtype=F32)[:, None] * inv_freq[None, :]
    cos = jnp.cos(ang)
    sin = jnp.sin(ang)
    return jnp.concatenate([cos, cos], axis=-1), jnp.concatenate([-sin, sin], axis=-1)


def _retention(proj, ret_norm):
    b, s, _ = proj.shape
    h, dh = RET_HEADS, HEAD_DIM
    cos, sin = _rope_tables(s, dh)
    log_gamma = jnp.log1p(-jnp.power(2.0, -5.0 - jnp.arange(h, dtype=F32)))
    lg = jnp.broadcast_to(log_gamma[:, None, None], (h, 1, dh))
    gn = ret_norm.astype(F32).reshape(h, 1, dh)

    def head_cols(group):
        return pl.BlockSpec((None, s, dh), lambda bi, hi: (bi, 0, group * h + hi))

    table = pl.BlockSpec((s, dh), lambda bi, hi: (0, 0))
    per_head = pl.BlockSpec((None, 1, dh), lambda bi, hi: (hi, 0, 0))
    return pl.pallas_call(
        _retention_body,
        grid=(b, h),
        in_specs=[head_cols(0), head_cols(1), head_cols(2), head_cols(3),
                  table, table, per_head, per_head],
        out_specs=pl.BlockSpec((None, s, dh), lambda bi, hi: (bi, 0, hi)),
        out_shape=jax.ShapeDtypeStruct((b, s, h * dh), BF16),
        compiler_params=_params(2),
        name="retention",
    )(proj, proj, proj, proj, cos, sin, lg, gn)


def _short_conv_body(b_ref, c_ref, u_ref, w_ref, o_ref):
    cu = c_ref[...] * u_ref[...]
    t = lax.broadcasted_iota(jnp.int32, cu.shape, 0)
    acc = w_ref[CONV_WIDTH - 1:CONV_WIDTH, :] * cu
    for lag in range(1, CONV_WIDTH):
        shifted = jnp.where(t >= lag, pltpu.roll(cu, lag, 0), 0.0)
        acc = acc + w_ref[CONV_WIDTH - 1 - lag:CONV_WIDTH - lag, :] * shifted
    o_ref[...] = (b_ref[...] * acc).astype(o_ref.dtype)


def _short_conv(proj, conv_w):
    b, s, _ = proj.shape
    d_conv = conv_w.shape[1]
    groups = d_conv // CONV_GROUP
    first = 4 * RET_HEADS * HEAD_DIM // CONV_GROUP

    def group_cols(which):
        return pl.BlockSpec((None, s, CONV_GROUP),
                            lambda bi, gi: (bi, 0, first + which * groups + gi))

    return pl.pallas_call(
        _short_conv_body,
        grid=(b, groups),
        in_specs=[group_cols(0), group_cols(1), group_cols(2),
                  pl.BlockSpec((CONV_WIDTH, CONV_GROUP), lambda bi, gi: (0, gi))],
        out_specs=pl.BlockSpec((None, s, CONV_GROUP), lambda bi, gi: (bi, 0, gi)),
        out_shape=jax.ShapeDtypeStruct((b, s, d_conv), BF16),
        compiler_params=_params(2),
        name="short_conv",
    )(proj, proj, proj, conv_w)


def kernel(x, norm_ffa, w_ffa_gate, w_ffa_up, w_ffa_down, norm_mix, w_in, conv_w, ret_norm,
           w_out, norm_ffb, w_ffb_gate, w_ffb_up, w_ffb_down, norm_final):
    batch, seq, d_model = x.shape
    depth = norm_ffa.shape[0]
    x = x.reshape(batch * seq, d_model)
    xg, r = _norm_split(x, norm_ffa[0])
    for l in range(depth):
        hidden, w_down16 = _gate_up(xg, r, w_ffa_gate, w_ffa_up, w_ffa_down, l)
        x, xg, r = _down(hidden, w_down16, x, next_gain=norm_mix[l])
        proj = _in_proj(xg, r, w_in, l).reshape(batch, seq, -1)
        ret_out = _retention(proj, ret_norm[l]).reshape(batch * seq, -1)
        conv_out = _short_conv(proj, conv_w[l]).reshape(batch * seq, -1)
        x, xg, r = _out_proj(ret_out, conv_out, w_out, x, l, norm_ffb[l])
        hidden, w_down16 = _gate_up(xg, r, w_ffb_gate, w_ffb_up, w_ffb_down, l)
        if l + 1 < depth:
            x, xg, r = _down(hidden, w_down16, x, next_gain=norm_ffa[l + 1])
        else:
            x = _down(hidden, w_down16, x)
    return _rmsnorm(x, norm_final).reshape(batch, seq, d_model)
```

```python
import functools

import jax
import jax.numpy as jnp
from jax import lax
from jax.experimental import pallas as pl
from jax.experimental.pallas import tpu as pltpu

F32 = jnp.float32
BF16 = jnp.bfloat16

RET_HEADS = 16
HEAD_DIM = 128
CONV_WIDTH = 3
CHUNK = 128
ROPE_THETA = 10000.0
EPS = 1e-6

LANES = 128
V7X_VMEM_LIMIT_BYTES = 60 * 1024 * 1024


def _params(n_grid_axes):
    return pltpu.CompilerParams(
        dimension_semantics=("arbitrary",) * n_grid_axes,
        vmem_limit_bytes=V7X_VMEM_LIMIT_BYTES,
    )


def _resident(block_shape, index_map):
    return pl.BlockSpec(block_shape, index_map, pipeline_mode=pl.Buffered(1))


def _tiles(total, tile):
    return [(o, tile) for o in range(0, total, tile)]


def _lane_tile(r, width):
    return r if width == LANES else jnp.concatenate([r] * (width // LANES), axis=1)


def _lane_block_sum(x):
    acc = x[:, :LANES]
    for c in range(LANES, x.shape[1], LANES):
        acc = acc + x[:, c:c + LANES]
    return acc


def _rmsnorm_body(x_ref, g_ref, o_ref):
    x = x_ref[...]
    y = x * lax.rsqrt(jnp.mean(x * x, axis=-1, keepdims=True) + EPS)
    o_ref[...] = (y * g_ref[...]).astype(o_ref.dtype)


def _rmsnorm(x, gain, block_rows=512):
    t, d = x.shape
    return pl.pallas_call(
        _rmsnorm_body,
        grid=(t // block_rows,),
        in_specs=[
            pl.BlockSpec((block_rows, d), lambda i: (i, 0)),
            pl.BlockSpec((1, d), lambda i: (0, 0)),
        ],
        out_specs=pl.BlockSpec((block_rows, d), lambda i: (i, 0)),
        out_shape=jax.ShapeDtypeStruct((t, d), F32),
        compiler_params=_params(1),
        name="rmsnorm",
    )(x, gain.reshape(1, d))


def _norm_split_body(x_ref, g_ref, xg_ref, r_ref):
    x = x_ref[...]
    xg_ref[...] = (x * g_ref[...]).astype(xg_ref.dtype)
    r = lax.rsqrt(jnp.mean(x * x, axis=-1, keepdims=True) + EPS)
    r_ref[...] = jnp.broadcast_to(r, r_ref.shape)


def _norm_split(x, gain, block_rows=512):
    t, d = x.shape
    return pl.pallas_call(
        _norm_split_body,
        grid=(t // block_rows,),
        in_specs=[
            pl.BlockSpec((block_rows, d), lambda i: (i, 0)),
            pl.BlockSpec((1, d), lambda i: (0, 0)),
        ],
        out_specs=[
            pl.BlockSpec((block_rows, d), lambda i: (i, 0)),
            pl.BlockSpec((block_rows, LANES), lambda i: (i, 0)),
        ],
        out_shape=[jax.ShapeDtypeStruct((t, d), BF16), jax.ShapeDtypeStruct((t, LANES), F32)],
        compiler_params=_params(1),
        name="norm_split",
    )(x, gain.reshape(1, d))


def _ssq_start(ssq_ref):
    @pl.when(pl.program_id(1) == 0)
    def _():
        ssq_ref[...] = jnp.zeros(ssq_ref.shape, ssq_ref.dtype)


def _emit_scaled_and_ssq(x_new, rows, cols, g_ref, xg_ref, ssq_ref):
    r0, rt = rows
    c0, ct = cols
    xg_ref[r0:r0 + rt, c0:c0 + ct] = (x_new * g_ref[:, c0:c0 + ct]).astype(xg_ref.dtype)
    ssq_ref[r0:r0 + rt, :] += _lane_block_sum(x_new * x_new)


def _ssq_finish(ssq_ref, r_ref, d_model):
    @pl.when(pl.program_id(1) == pl.num_programs(1) - 1)
    def _():
        total = jnp.sum(ssq_ref[...], axis=-1, keepdims=True)
        r = lax.rsqrt(total * (1.0 / d_model) + EPS)
        r_ref[...] = jnp.broadcast_to(r, r_ref.shape)


def _gate_up_body(a_ref, r_ref, wg_ref, wu_ref, o_ref, *, row_tile, col_tile):
    bm, bn = o_ref.shape
    wg = wg_ref[...].astype(BF16)
    wu = wu_ref[...].astype(BF16)
    for r0, rt in _tiles(bm, row_tile):
        a = a_ref[r0:r0 + rt, :]
        for c0, ct in _tiles(bn, col_tile):
            scale = _lane_tile(r_ref[r0:r0 + rt, :], ct)
            gate = jnp.dot(a, wg[:, c0:c0 + ct], preferred_element_type=F32) * scale
            up = jnp.dot(a, wu[:, c0:c0 + ct], preferred_element_type=F32) * scale
            o_ref[r0:r0 + rt, c0:c0 + ct] = (jax.nn.silu(gate) * up).astype(o_ref.dtype)


def _gate_up(xg, r, w_gate, w_up, layer, bm=2048, bn=256, row_tile=512, col_tile=256):
    t, d = xg.shape
    ff = w_gate.shape[2]
    body = functools.partial(_gate_up_body, row_tile=row_tile, col_tile=col_tile)
    return pl.pallas_call(
        body,
        grid=(t // bm, ff // bn),
        in_specs=[
            pl.BlockSpec((bm, d), lambda i, j: (i, 0)),
            _resident((bm, LANES), lambda i, j: (i, 0)),
            pl.BlockSpec((None, d, bn), lambda i, j: (layer, 0, j)),
            pl.BlockSpec((None, d, bn), lambda i, j: (layer, 0, j)),
        ],
        out_specs=pl.BlockSpec((bm, bn), lambda i, j: (i, j)),
        out_shape=jax.ShapeDtypeStruct((t, ff), BF16),
        compiler_params=_params(2),
        name="gate_up",
    )(xg, r, w_gate, w_up)


def _down_body(a_ref, w_ref, res_ref, *rest, d_model, emit_norm, col_tile):
    if emit_norm:
        g_ref, o_ref, xg_ref, r_ref, ssq_ref = rest
        _ssq_start(ssq_ref)
    else:
        (o_ref,) = rest
    bm, bn = o_ref.shape
    a = a_ref[...]
    w = w_ref[...].astype(BF16)
    for c0, ct in _tiles(bn, col_tile):
        x_new = res_ref[:, c0:c0 + ct] + 0.5 * jnp.dot(a, w[:, c0:c0 + ct],
                                                      preferred_element_type=F32)
        o_ref[:, c0:c0 + ct] = x_new
        if emit_norm:
            _emit_scaled_and_ssq(x_new, (0, bm), (c0, ct), g_ref, xg_ref, ssq_ref)
    if emit_norm:
        _ssq_finish(ssq_ref, r_ref, d_model)


def _down(a, w, resid, layer, next_gain=None, bm=1024, bn=256, col_tile=256):
    t, ff = a.shape
    d = w.shape[2]
    emit_norm = next_gain is not None
    in_specs = [
        _resident((bm, ff), lambda i, j: (i, 0)),
        pl.BlockSpec((None, ff, bn), lambda i, j: (layer, 0, j)),
        pl.BlockSpec((bm, bn), lambda i, j: (i, j)),
    ]
    out_specs = [pl.BlockSpec((bm, bn), lambda i, j: (i, j))]
    out_shape = [jax.ShapeDtypeStruct((t, d), F32)]
    args = [a, w, resid]
    scratch = []
    if emit_norm:
        in_specs.append(pl.BlockSpec((1, bn), lambda i, j: (0, j)))
        out_specs += [pl.BlockSpec((bm, bn), lambda i, j: (i, j)),
                      pl.BlockSpec((bm, LANES), lambda i, j: (i, 0))]
        out_shape += [jax.ShapeDtypeStruct((t, d), BF16), jax.ShapeDtypeStruct((t, LANES), F32)]
        args.append(next_gain.reshape(1, d))
        scratch = [pltpu.VMEM((bm, LANES), F32)]
    out = pl.pallas_call(
        functools.partial(_down_body, d_model=d, emit_norm=emit_norm, col_tile=col_tile),
        grid=(t // bm, d // bn),
        in_specs=in_specs,
        out_specs=out_specs,
        out_shape=out_shape,
        scratch_shapes=scratch,
        compiler_params=_params(2),
        name="down",
    )(*args)
    return out if emit_norm else out[0]


def _in_proj_body(a_ref, r_ref, w_ref, o_ref, *, row_tile, col_tile):
    bm, bn = o_ref.shape
    w = w_ref[...].astype(BF16)
    for r0, rt in _tiles(bm, row_tile):
        a = a_ref[r0:r0 + rt, :]
        for c0, ct in _tiles(bn, col_tile):
            scale = _lane_tile(r_ref[r0:r0 + rt, :], ct)
            o_ref[r0:r0 + rt, c0:c0 + ct] = (
                jnp.dot(a, w[:, c0:c0 + ct], preferred_element_type=F32) * scale)


def _in_proj(xg, r, w, layer, bm=2048, bn=512, row_tile=512, col_tile=256):
    t, d = xg.shape
    n = w.shape[2]
    body = functools.partial(_in_proj_body, row_tile=row_tile, col_tile=col_tile)
    return pl.pallas_call(
        body,
        grid=(t // bm, n // bn),
        in_specs=[
            _resident((bm, d), lambda i, j: (i, 0)),
            _resident((bm, LANES), lambda i, j: (i, 0)),
            pl.BlockSpec((None, d, bn), lambda i, j: (layer, 0, j)),
        ],
        out_specs=pl.BlockSpec((bm, bn), lambda i, j: (i, j)),
        out_shape=jax.ShapeDtypeStruct((t, n), F32),
        compiler_params=_params(2),
        name="in_proj",
    )(xg, r, w)


def _out_proj_body(a1_ref, a2_ref, w1_ref, w2_ref, res_ref, g_ref, o_ref, xg_ref, r_ref, ssq_ref,
                   *, row_tile, col_tile, d_model):
    bm, bn = o_ref.shape
    _ssq_start(ssq_ref)
    w1 = w1_ref[...].astype(BF16)
    w2 = w2_ref[...].astype(BF16)
    for r0, rt in _tiles(bm, row_tile):
        a1 = a1_ref[r0:r0 + rt, :]
        a2 = a2_ref[r0:r0 + rt, :]
        for c0, ct in _tiles(bn, col_tile):
            acc = (jnp.dot(a1, w1[:, c0:c0 + ct], preferred_element_type=F32)
                   + jnp.dot(a2, w2[:, c0:c0 + ct], preferred_element_type=F32))
            x_new = res_ref[r0:r0 + rt, c0:c0 + ct] + acc
            o_ref[r0:r0 + rt, c0:c0 + ct] = x_new
            _emit_scaled_and_ssq(x_new, (r0, rt), (c0, ct), g_ref, xg_ref, ssq_ref)
    _ssq_finish(ssq_ref, r_ref, d_model)


def _out_proj(a_ret, a_conv, w, resid, layer, next_gain, bm=2048, bn=512, row_tile=1024,
              col_tile=256):
    t, d_ret = a_ret.shape
    d_conv = a_conv.shape[1]
    assert d_ret == d_conv
    n = w.shape[2]
    body = functools.partial(_out_proj_body, row_tile=row_tile, col_tile=col_tile, d_model=n)
    return pl.pallas_call(
        body,
        grid=(t // bm, n // bn),
        in_specs=[
            _resident((bm, d_ret), lambda i, j: (i, 0)),
            _resident((bm, d_conv), lambda i, j: (i, 0)),
            pl.BlockSpec((None, d_ret, bn), lambda i, j: (layer, 0, j)),
            pl.BlockSpec((None, d_conv, bn), lambda i, j: (layer, 1, j)),
            pl.BlockSpec((bm, bn), lambda i, j: (i, j)),
            pl.BlockSpec((1, bn), lambda i, j: (0, j)),
        ],
        out_specs=[
            pl.BlockSpec((bm, bn), lambda i, j: (i, j)),
            pl.BlockSpec((bm, bn), lambda i, j: (i, j)),
            pl.BlockSpec((bm, LANES), lambda i, j: (i, 0)),
        ],
        out_shape=[jax.ShapeDtypeStruct((t, n), F32), jax.ShapeDtypeStruct((t, n), BF16),
                   jax.ShapeDtypeStruct((t, LANES), F32)],
        scratch_shapes=[pltpu.VMEM((bm, LANES), F32)],
        compiler_params=_params(2),
        name="out_proj",
    )(a_ret, a_conv, w, w, resid, next_gain.reshape(1, n))


def _block_diag(top_left, bottom_right):
    zero = jnp.zeros(top_left.shape, top_left.dtype)
    return jnp.concatenate([jnp.concatenate([top_left, zero], axis=1),
                            jnp.concatenate([zero, bottom_right], axis=1)], axis=0)


def _retention_body(q_ref, k_ref, v_ref, g_ref, cos_ref, sin_ref, lg_ref, gn_ref, o_ref):
    s, width = q_ref.shape
    dh = HEAD_DIM
    assert width == 2 * dh and CHUNK == dh
    n_chunks = s // CHUNK
    lg = lg_ref[...]

    row = lax.broadcasted_iota(jnp.int32, (CHUNK, width), 0)
    key = lax.broadcasted_iota(jnp.int32, (CHUNK, width), 1) % CHUNK
    diff = (row - key).astype(F32)
    intra = jnp.where(diff >= 0, jnp.exp(lg * jnp.maximum(diff, 0.0)), 0.0)
    idx = row.astype(F32)
    zeta = jnp.exp(lg * (CHUNK - 1.0 - idx))
    xi = jnp.exp(lg * (idx + 1.0))
    chunk_decay = jnp.exp(lg * float(CHUNK))
    heads = (slice(0, dh), slice(dh, width))

    def rope(x, cos, sin):
        return jnp.concatenate(
            [x[:, h] * cos + pltpu.roll(x[:, h], dh // 2, 1) * sin for h in heads], axis=1)

    states = [jnp.zeros((dh, dh), F32) for _ in heads]
    for c in range(n_chunks):
        rows = pl.ds(c * CHUNK, CHUNK)
        cos = cos_ref[rows, :]
        sin = sin_ref[rows, :]
        q = rope(q_ref[rows, :], cos, sin)
        k = rope(k_ref[rows, :], cos, sin) * (dh ** -0.5)
        q16 = q.astype(BF16)
        k16 = k.astype(BF16)
        v16 = v_ref[rows, :].astype(BF16)

        scores = lax.dot_general(q16, _block_diag(k16[:, heads[0]], k16[:, heads[1]]),
                                 (((1,), (1,)), ((), ())), preferred_element_type=F32) * intra
        lhs = jnp.concatenate([scores.astype(BF16), (q * xi).astype(BF16)], axis=1)
        rhs = jnp.concatenate(
            [_block_diag(v16[:, heads[0]], v16[:, heads[1]]),
             _block_diag(states[0].astype(BF16), states[1].astype(BF16))], axis=0)
        out = jnp.dot(lhs, rhs, preferred_element_type=F32)
        kv = lax.dot_general((k * zeta).astype(BF16), v16, (((0,), (0,)), ((), ())),
                             preferred_element_type=F32)
        for i, h in enumerate(heads):
            states[i] = states[i] * chunk_decay[:, h] + kv[h, h]
            o = out[:, h]
            o = o * lax.rsqrt(jnp.mean(o * o, axis=-1, keepdims=True) + EPS)
            o = o * gn_ref[:, h]
            o_ref[rows, h] = (jax.nn.silu(g_ref[rows, h]) * o).astype(o_ref.dtype)


def _rope_tables(seq, dh):
    half = dh // 2
    inv_freq = ROPE_THETA ** (-jnp.arange(half, dtype=F32) / half)
    ang = jnp.arange(seq, dtype=F32)[:, None] * inv_freq[None, :]
    cos = jnp.cos(ang)
    sin = jnp.sin(ang)
    return jnp.concatenate([cos, cos], axis=-1), jnp.concatenate([-sin, sin], axis=-1)


def _retention(proj, ret_norm, heads_per_step=2):
    b, s, _ = proj.shape
    h, dh = RET_HEADS, HEAD_DIM
    width = heads_per_step * dh
    steps = h // heads_per_step
    cos, sin = _rope_tables(s, dh)
    log_gamma = jnp.log1p(-jnp.power(2.0, -5.0 - jnp.arange(h, dtype=F32)))
    lg = jnp.repeat(log_gamma, dh).reshape(1, h * dh)
    gn = ret_norm.astype(F32).reshape(1, h * dh)

    def head_cols(group):
        return pl.BlockSpec((None, s, width), lambda bi, hi: (bi, 0, group * steps + hi))

    table = pl.BlockSpec((s, dh), lambda bi, hi: (0, 0))
    per_head = pl.BlockSpec((1, width), lambda bi, hi: (0, hi))
    return pl.pallas_call(
        _retention_body,
        grid=(b, steps),
        in_specs=[head_cols(0), head_cols(1), head_cols(2), head_cols(3),
                  table, table, per_head, per_head],
        out_specs=pl.BlockSpec((None, s, width), lambda bi, hi: (bi, 0, hi)),
        out_shape=jax.ShapeDtypeStruct((b, s, h * dh), BF16),
        compiler_params=_params(2),
        name="retention",
    )(proj, proj, proj, proj, cos, sin, lg, gn)


def _short_conv_body(b_ref, c_ref, u_ref, w_ref, o_ref):
    cu = c_ref[...] * u_ref[...]
    t = lax.broadcasted_iota(jnp.int32, cu.shape, 0)
    acc = w_ref[CONV_WIDTH - 1:CONV_WIDTH, :] * cu
    for lag in range(1, CONV_WIDTH):
        shifted = jnp.where(t >= lag, pltpu.roll(cu, lag, 0), 0.0)
        acc = acc + w_ref[CONV_WIDTH - 1 - lag:CONV_WIDTH - lag, :] * shifted
    o_ref[...] = (b_ref[...] * acc).astype(o_ref.dtype)


def _short_conv(proj, conv_w, channels_per_step=512):
    b, s, _ = proj.shape
    d_conv = conv_w.shape[1]
    steps = d_conv // channels_per_step
    first = 4 * RET_HEADS * HEAD_DIM // channels_per_step

    def block_cols(which):
        return pl.BlockSpec((None, s, channels_per_step),
                            lambda bi, ci: (bi, 0, first + which * steps + ci))

    return pl.pallas_call(
        _short_conv_body,
        grid=(b, steps),
        in_specs=[block_cols(0), block_cols(1), block_cols(2),
                  pl.BlockSpec((CONV_WIDTH, channels_per_step), lambda bi, ci: (0, ci))],
        out_specs=pl.BlockSpec((None, s, channels_per_step), lambda bi, ci: (bi, 0, ci)),
        out_shape=jax.ShapeDtypeStruct((b, s, d_conv), BF16),
        compiler_params=_params(2),
        name="short_conv",
    )(proj, proj, proj, conv_w)


def kernel(x, norm_ffa, w_ffa_gate, w_ffa_up, w_ffa_down, norm_mix, w_in, conv_w, ret_norm,
           w_out, norm_ffb, w_ffb_gate, w_ffb_up, w_ffb_down, norm_final):
    batch, seq, d_model = x.shape
    depth = norm_ffa.shape[0]
    x = x.reshape(batch * seq, d_model)
    xg, r = _norm_split(x, norm_ffa[0])
    for l in range(depth):
        hidden = _gate_up(xg, r, w_ffa_gate, w_ffa_up, l)
        x, xg, r = _down(hidden, w_ffa_down, x, l, next_gain=norm_mix[l])
        proj = _in_proj(xg, r, w_in, l).reshape(batch, seq, -1)
        ret_out = _retention(proj, ret_norm[l]).reshape(batch * seq, -1)
        conv_out = _short_conv(proj, conv_w[l]).reshape(batch * seq, -1)
        x, xg, r = _out_proj(ret_out, conv_out, w_out, x, l, norm_ffb[l])
        hidden = _gate_up(xg, r, w_ffb_gate, w_ffb_up, l)
        if l + 1 < depth:
            x, xg, r = _down(hidden, w_ffb_down, x, l, next_gain=norm_ffa[l + 1])
        else:
            x = _down(hidden, w_ffb_down, x, l)
    return _rmsnorm(x, norm_final).reshape(batch, seq, d_model)
```

```python
import functools

import jax
import jax.numpy as jnp
from jax import lax
from jax.experimental import pallas as pl
from jax.experimental.pallas import tpu as pltpu

F32 = jnp.float32
BF16 = jnp.bfloat16

RET_HEADS = 16
HEAD_DIM = 128
CONV_WIDTH = 3
CHUNK = 128
ROPE_THETA = 10000.0
EPS = 1e-6

LANES = 128
V7X_VMEM_LIMIT_BYTES = 60 * 1024 * 1024


def _params(n_grid_axes, vmem_limit_bytes=V7X_VMEM_LIMIT_BYTES):
    return pltpu.CompilerParams(
        dimension_semantics=("arbitrary",) * n_grid_axes,
        vmem_limit_bytes=vmem_limit_bytes,
    )


def _resident(block_shape, index_map):
    return pl.BlockSpec(block_shape, index_map, pipeline_mode=pl.Buffered(1))


def _tiles(total, tile):
    return [(o, tile) for o in range(0, total, tile)]


def _lane_tile(r, width):
    return r if width == LANES else jnp.concatenate([r] * (width // LANES), axis=1)


def _lane_block_sum(x):
    acc = x[:, :LANES]
    for c in range(LANES, x.shape[1], LANES):
        acc = acc + x[:, c:c + LANES]
    return acc


def _rmsnorm_body(x_ref, g_ref, o_ref):
    x = x_ref[...]
    y = x * lax.rsqrt(jnp.mean(x * x, axis=-1, keepdims=True) + EPS)
    o_ref[...] = (y * g_ref[...]).astype(o_ref.dtype)


def _rmsnorm(x, gain, block_rows=512):
    t, d = x.shape
    return pl.pallas_call(
        _rmsnorm_body,
        grid=(t // block_rows,),
        in_specs=[
            pl.BlockSpec((block_rows, d), lambda i: (i, 0)),
            pl.BlockSpec((1, d), lambda i: (0, 0)),
        ],
        out_specs=pl.BlockSpec((block_rows, d), lambda i: (i, 0)),
        out_shape=jax.ShapeDtypeStruct((t, d), F32),
        compiler_params=_params(1),
        name="rmsnorm",
    )(x, gain.reshape(1, d))


def _norm_split_body(x_ref, g_ref, xg_ref, r_ref):
    x = x_ref[...]
    xg_ref[...] = (x * g_ref[...]).astype(xg_ref.dtype)
    r = lax.rsqrt(jnp.mean(x * x, axis=-1, keepdims=True) + EPS)
    r_ref[...] = jnp.broadcast_to(r, r_ref.shape)


def _norm_split(x, gain, block_rows=512):
    t, d = x.shape
    return pl.pallas_call(
        _norm_split_body,
        grid=(t // block_rows,),
        in_specs=[
            pl.BlockSpec((block_rows, d), lambda i: (i, 0)),
            pl.BlockSpec((1, d), lambda i: (0, 0)),
        ],
        out_specs=[
            pl.BlockSpec((block_rows, d), lambda i: (i, 0)),
            pl.BlockSpec((block_rows, LANES), lambda i: (i, 0)),
        ],
        out_shape=[jax.ShapeDtypeStruct((t, d), BF16), jax.ShapeDtypeStruct((t, LANES), F32)],
        compiler_params=_params(1),
        name="norm_split",
    )(x, gain.reshape(1, d))


def _ssq_start(ssq_ref):
    @pl.when(pl.program_id(1) == 0)
    def _():
        ssq_ref[...] = jnp.zeros(ssq_ref.shape, ssq_ref.dtype)


def _emit_scaled_and_ssq(x_new, rows, cols, g_ref, xg_ref, ssq_ref):
    r0, rt = rows
    c0, ct = cols
    xg_ref[r0:r0 + rt, c0:c0 + ct] = (x_new * g_ref[:, c0:c0 + ct]).astype(xg_ref.dtype)
    ssq_ref[r0:r0 + rt, :] += _lane_block_sum(x_new * x_new)


def _ssq_finish(ssq_ref, r_ref, d_model):
    @pl.when(pl.program_id(1) == pl.num_programs(1) - 1)
    def _():
        total = jnp.sum(ssq_ref[...], axis=-1, keepdims=True)
        r = lax.rsqrt(total * (1.0 / d_model) + EPS)
        r_ref[...] = jnp.broadcast_to(r, r_ref.shape)


def _gate_up_body(a_ref, r_ref, wg_ref, wu_ref, o_ref, *, row_tile, col_tile):
    bm, bn = o_ref.shape
    wg = wg_ref[...].astype(BF16)
    wu = wu_ref[...].astype(BF16)
    for r0, rt in _tiles(bm, row_tile):
        a = a_ref[r0:r0 + rt, :]
        for c0, ct in _tiles(bn, col_tile):
            scale = _lane_tile(r_ref[r0:r0 + rt, :], ct)
            gate = jnp.dot(a, wg[:, c0:c0 + ct], preferred_element_type=F32) * scale
            up = jnp.dot(a, wu[:, c0:c0 + ct], preferred_element_type=F32) * scale
            o_ref[r0:r0 + rt, c0:c0 + ct] = (jax.nn.silu(gate) * up).astype(o_ref.dtype)


def _gate_up(xg, r, w_gate, w_up, layer, bm=2048, bn=256, row_tile=512, col_tile=256):
    t, d = xg.shape
    ff = w_gate.shape[2]
    body = functools.partial(_gate_up_body, row_tile=row_tile, col_tile=col_tile)
    return pl.pallas_call(
        body,
        grid=(t // bm, ff // bn),
        in_specs=[
            pl.BlockSpec((bm, d), lambda i, j: (i, 0)),
            _resident((bm, LANES), lambda i, j: (i, 0)),
            pl.BlockSpec((None, d, bn), lambda i, j: (layer, 0, j)),
            pl.BlockSpec((None, d, bn), lambda i, j: (layer, 0, j)),
        ],
        out_specs=pl.BlockSpec((bm, bn), lambda i, j: (i, j)),
        out_shape=jax.ShapeDtypeStruct((t, ff), BF16),
        compiler_params=_params(2),
        name="gate_up",
    )(xg, r, w_gate, w_up)


def _down_body(a_ref, a2_ref, w_ref, res_ref, *rest, d_model, emit_norm, col_tile):
    if emit_norm:
        g_ref, o_ref, xg_ref, r_ref, ssq_ref = rest
        _ssq_start(ssq_ref)
    else:
        (o_ref,) = rest
    bm, bn = o_ref.shape
    half = a_ref.shape[1]
    w = w_ref[...].astype(BF16)
    for c0, ct in _tiles(bn, col_tile):
        acc = (jnp.dot(a_ref[...], w[:half, c0:c0 + ct], preferred_element_type=F32)
               + jnp.dot(a2_ref[...], w[half:, c0:c0 + ct], preferred_element_type=F32))
        x_new = res_ref[:, c0:c0 + ct] + 0.5 * acc
        o_ref[:, c0:c0 + ct] = x_new
        if emit_norm:
            _emit_scaled_and_ssq(x_new, (0, bm), (c0, ct), g_ref, xg_ref, ssq_ref)
    if emit_norm:
        _ssq_finish(ssq_ref, r_ref, d_model)


def _down(a, w, resid, layer, next_gain=None, bm=1024, bn=256, col_tile=256):
    t, ff = a.shape
    d = w.shape[2]
    emit_norm = next_gain is not None
    in_specs = [
        pl.BlockSpec((bm, ff // 2), lambda i, j: (i, 0)),
        _resident((bm, ff // 2), lambda i, j: (i, 1)),
        pl.BlockSpec((None, ff, bn), lambda i, j: (layer, 0, j)),
        pl.BlockSpec((bm, bn), lambda i, j: (i, j)),
    ]
    out_specs = [pl.BlockSpec((bm, bn), lambda i, j: (i, j))]
    out_shape = [jax.ShapeDtypeStruct((t, d), F32)]
    args = [a, a, w, resid]
    scratch = []
    if emit_norm:
        in_specs.append(pl.BlockSpec((1, bn), lambda i, j: (0, j)))
        out_specs += [pl.BlockSpec((bm, bn), lambda i, j: (i, j)),
                      pl.BlockSpec((bm, LANES), lambda i, j: (i, 0))]
        out_shape += [jax.ShapeDtypeStruct((t, d), BF16), jax.ShapeDtypeStruct((t, LANES), F32)]
        args.append(next_gain.reshape(1, d))
        scratch = [pltpu.VMEM((bm, LANES), F32)]
    out = pl.pallas_call(
        functools.partial(_down_body, d_model=d, emit_norm=emit_norm, col_tile=col_tile),
        grid=(t // bm, d // bn),
        in_specs=in_specs,
        out_specs=out_specs,
        out_shape=out_shape,
        scratch_shapes=scratch,
        compiler_params=_params(2, vmem_limit_bytes=62 * 1024 * 1024),
        name="down",
    )(*args)
    return out if emit_norm else out[0]


def _in_proj_body(a_ref, r_ref, w_ref, o_ref, *, row_tile, col_tile):
    bm, bn = o_ref.shape
    w = w_ref[...].astype(BF16)
    for r0, rt in _tiles(bm, row_tile):
        a = a_ref[r0:r0 + rt, :]
        for c0, ct in _tiles(bn, col_tile):
            scale = _lane_tile(r_ref[r0:r0 + rt, :], ct)
            o_ref[r0:r0 + rt, c0:c0 + ct] = (
                jnp.dot(a, w[:, c0:c0 + ct], preferred_element_type=F32) * scale)


def _in_proj(xg, r, w, layer, bm=2048, bn=512, row_tile=512, col_tile=256):
    t, d = xg.shape
    n = w.shape[2]
    body = functools.partial(_in_proj_body, row_tile=row_tile, col_tile=col_tile)
    return pl.pallas_call(
        body,
        grid=(t // bm, n // bn),
        in_specs=[
            _resident((bm, d), lambda i, j: (i, 0)),
            _resident((bm, LANES), lambda i, j: (i, 0)),
            pl.BlockSpec((None, d, bn), lambda i, j: (layer, 0, j)),
        ],
        out_specs=pl.BlockSpec((bm, bn), lambda i, j: (i, j)),
        out_shape=jax.ShapeDtypeStruct((t, n), F32),
        compiler_params=_params(2),
        name="in_proj",
    )(xg, r, w)


def _out_proj_body(a1_ref, a2_ref, w1_ref, w2_ref, res_ref, g_ref, o_ref, xg_ref, r_ref, ssq_ref,
                   *, row_tile, col_tile, d_model):
    bm, bn = o_ref.shape
    _ssq_start(ssq_ref)
    w1 = w1_ref[...].astype(BF16)
    w2 = w2_ref[...].astype(BF16)
    for r0, rt in _tiles(bm, row_tile):
        a1 = a1_ref[r0:r0 + rt, :]
        a2 = a2_ref[r0:r0 + rt, :]
        for c0, ct in _tiles(bn, col_tile):
            acc = (jnp.dot(a1, w1[:, c0:c0 + ct], preferred_element_type=F32)
                   + jnp.dot(a2, w2[:, c0:c0 + ct], preferred_element_type=F32))
            x_new = res_ref[r0:r0 + rt, c0:c0 + ct] + acc
            o_ref[r0:r0 + rt, c0:c0 + ct] = x_new
            _emit_scaled_and_ssq(x_new, (r0, rt), (c0, ct), g_ref, xg_ref, ssq_ref)
    _ssq_finish(ssq_ref, r_ref, d_model)


def _out_proj(a_ret, a_conv, w, resid, layer, next_gain, bm=2048, bn=512, row_tile=1024,
              col_tile=256):
    t, d_ret = a_ret.shape
    d_conv = a_conv.shape[1]
    assert d_ret == d_conv
    n = w.shape[2]
    body = functools.partial(_out_proj_body, row_tile=row_tile, col_tile=col_tile, d_model=n)
    return pl.pallas_call(
        body,
        grid=(t // bm, n // bn),
        in_specs=[
            _resident((bm, d_ret), lambda i, j: (i, 0)),
            _resident((bm, d_conv), lambda i, j: (i, 0)),
            pl.BlockSpec((None, d_ret, bn), lambda i, j: (layer, 0, j)),
            pl.BlockSpec((None, d_conv, bn), lambda i, j: (layer, 1, j)),
            pl.BlockSpec((bm, bn), lambda i, j: (i, j)),
            pl.BlockSpec((1, bn), lambda i, j: (0, j)),
        ],
        out_specs=[
            pl.BlockSpec((bm, bn), lambda i, j: (i, j)),
            pl.BlockSpec((bm, bn), lambda i, j: (i, j)),
            pl.BlockSpec((bm, LANES), lambda i, j: (i, 0)),
        ],
        out_shape=[jax.ShapeDtypeStruct((t, n), F32), jax.ShapeDtypeStruct((t, n), BF16),
                   jax.ShapeDtypeStruct((t, LANES), F32)],
        scratch_shapes=[pltpu.VMEM((bm, LANES), F32)],
        compiler_params=_params(2),
        name="out_proj",
    )(a_ret, a_conv, w, w, resid, next_gain.reshape(1, n))


def _block_diag(top_left, bottom_right):
    zero = jnp.zeros(top_left.shape, top_left.dtype)
    return jnp.concatenate([jnp.concatenate([top_left, zero], axis=1),
                            jnp.concatenate([zero, bottom_right], axis=1)], axis=0)


def _retention_body(q_ref, k_ref, v_ref, g_ref, cos_ref, sin_ref, lg_ref, gn_ref, o_ref):
    s, width = q_ref.shape
    dh = HEAD_DIM
    assert width == 2 * dh and CHUNK == dh
    n_chunks = s // CHUNK
    lg = lg_ref[...]

    row = lax.broadcasted_iota(jnp.int32, (CHUNK, width), 0)
    key = lax.broadcasted_iota(jnp.int32, (CHUNK, width), 1) % CHUNK
    diff = (row - key).astype(F32)
    intra = jnp.where(diff >= 0, jnp.exp(lg * jnp.maximum(diff, 0.0)), 0.0)
    idx = row.astype(F32)
    zeta = jnp.exp(lg * (CHUNK - 1.0 - idx))
    xi = jnp.exp(lg * (idx + 1.0))
    chunk_decay = jnp.exp(lg * float(CHUNK))
    heads = (slice(0, dh), slice(dh, width))

    def rope(x, cos, sin):
        return jnp.concatenate(
            [x[:, h] * cos + pltpu.roll(x[:, h], dh // 2, 1) * sin for h in heads], axis=1)

    states = [jnp.zeros((dh, dh), F32) for _ in heads]
    for c in range(n_chunks):
        rows = pl.ds(c * CHUNK, CHUNK)
        cos = cos_ref[rows, :]
        sin = sin_ref[rows, :]
        q = rope(q_ref[rows, :], cos, sin)
        k = rope(k_ref[rows, :], cos, sin) * (dh ** -0.5)
        q16 = q.astype(BF16)
        k16 = k.astype(BF16)
        v16 = v_ref[rows, :].astype(BF16)

        scores = lax.dot_general(q16, _block_diag(k16[:, heads[0]], k16[:, heads[1]]),
                                 (((1,), (1,)), ((), ())), preferred_element_type=F32) * intra
        lhs = jnp.concatenate([scores.astype(BF16), (q * xi).astype(BF16)], axis=1)
        rhs = jnp.concatenate(
            [_block_diag(v16[:, heads[0]], v16[:, heads[1]]),
             _block_diag(states[0].astype(BF16), states[1].astype(BF16))], axis=0)
        out = jnp.dot(lhs, rhs, preferred_element_type=F32)
        kv = lax.dot_general((k * zeta).astype(BF16), v16, (((0,), (0,)), ((), ())),
                             preferred_element_type=F32)
        for i, h in enumerate(heads):
            states[i] = states[i] * chunk_decay[:, h] + kv[h, h]
            o = out[:, h]
            o = o * lax.rsqrt(jnp.mean(o * o, axis=-1, keepdims=True) + EPS)
            o = o * gn_ref[:, h]
            o_ref[rows, h] = (jax.nn.silu(g_ref[rows, h]) * o).astype(o_ref.dtype)


def _rope_tables(seq, dh):
    half = dh // 2
    inv_freq = ROPE_THETA ** (-jnp.arange(half, dtype=F32) / half)
    ang = jnp.arange(seq, dtype=F32)[:, None] * inv_freq[None, :]
    cos = jnp.cos(ang)
    sin = jnp.sin(ang)
    return jnp.concatenate([cos, cos], axis=-1), jnp.concatenate([-sin, sin], axis=-1)


def _retention(proj, ret_norm, heads_per_step=2):
    b, s, _ = proj.shape
    h, dh = RET_HEADS, HEAD_DIM
    width = heads_per_step * dh
    steps = h // heads_per_step
    cos, sin = _rope_tables(s, dh)
    log_gamma = jnp.log1p(-jnp.power(2.0, -5.0 - jnp.arange(h, dtype=F32)))
    lg = jnp.repeat(log_gamma, dh).reshape(1, h * dh)
    gn = ret_norm.astype(F32).reshape(1, h * dh)

    def head_cols(group):
        return pl.BlockSpec((None, s, width), lambda bi, hi: (bi, 0, group * steps + hi))

    table = pl.BlockSpec((s, dh), lambda bi, hi: (0, 0))
    per_head = pl.BlockSpec((1, width), lambda bi, hi: (0, hi))
    return pl.pallas_call(
        _retention_body,
        grid=(b, steps),
        in_specs=[head_cols(0), head_cols(1), head_cols(2), head_cols(3),
                  table, table, per_head, per_head],
        out_specs=pl.BlockSpec((None, s, width), lambda bi, hi: (bi, 0, hi)),
        out_shape=jax.ShapeDtypeStruct((b, s, h * dh), BF16),
        compiler_params=_params(2),
        name="retention",
    )(proj, proj, proj, proj, cos, sin, lg, gn)


def _short_conv_body(b_ref, c_ref, u_ref, w_ref, o_ref):
    cu = c_ref[...] * u_ref[...]
    t = lax.broadcasted_iota(jnp.int32, cu.shape, 0)
    acc = w_ref[CONV_WIDTH - 1:CONV_WIDTH, :] * cu
    for lag in range(1, CONV_WIDTH):
        shifted = jnp.where(t >= lag, pltpu.roll(cu, lag, 0), 0.0)
        acc = acc + w_ref[CONV_WIDTH - 1 - lag:CONV_WIDTH - lag, :] * shifted
    o_ref[...] = (b_ref[...] * acc).astype(o_ref.dtype)


def _short_conv(proj, conv_w, channels_per_step=512):
    b, s, _ = proj.shape
    d_conv = conv_w.shape[1]
    steps = d_conv // channels_per_step
    first = 4 * RET_HEADS * HEAD_DIM // channels_per_step

    def block_cols(which):
        return pl.BlockSpec((None, s, channels_per_step),
                            lambda bi, ci: (bi, 0, first + which * steps + ci))

    return pl.pallas_call(
        _short_conv_body,
        grid=(b, steps),
        in_specs=[block_cols(0), block_cols(1), block_cols(2),
                  pl.BlockSpec((CONV_WIDTH, channels_per_step), lambda bi, ci: (0, ci))],
        out_specs=pl.BlockSpec((None, s, channels_per_step), lambda bi, ci: (bi, 0, ci)),
        out_shape=jax.ShapeDtypeStruct((b, s, d_conv), BF16),
        compiler_params=_params(2),
        name="short_conv",
    )(proj, proj, proj, conv_w)


def kernel(x, norm_ffa, w_ffa_gate, w_ffa_up, w_ffa_down, norm_mix, w_in, conv_w, ret_norm,
           w_out, norm_ffb, w_ffb_gate, w_ffb_up, w_ffb_down, norm_final):
    batch, seq, d_model = x.shape
    depth = norm_ffa.shape[0]
    x = x.reshape(batch * seq, d_model)
    xg, r = _norm_split(x, norm_ffa[0])
    for l in range(depth):
        hidden = _gate_up(xg, r, w_ffa_gate, w_ffa_up, l)
        x, xg, r = _down(hidden, w_ffa_down, x, l, next_gain=norm_mix[l])
        proj = _in_proj(xg, r, w_in, l).reshape(batch, seq, -1)
        ret_out = _retention(proj, ret_norm[l]).reshape(batch * seq, -1)
        conv_out = _short_conv(proj, conv_w[l]).reshape(batch * seq, -1)
        x, xg, r = _out_proj(ret_out, conv_out, w_out, x, l, norm_ffb[l])
        hidden = _gate_up(xg, r, w_ffb_gate, w_ffb_up, l)
        if l + 1 < depth:
            x, xg, r = _down(hidden, w_ffb_down, x, l, next_gain=norm_ffa[l + 1])
        else:
            x = _down(hidden, w_ffb_down, x, l)
    return _rmsnorm(x, norm_final).reshape(batch, seq, d_model)
```

```python
import functools

import jax
import jax.numpy as jnp
from jax import lax
from jax.experimental import pallas as pl
from jax.experimental.pallas import tpu as pltpu

F32 = jnp.float32
BF16 = jnp.bfloat16

RET_HEADS = 16
HEAD_DIM = 128
CONV_WIDTH = 3
CHUNK = 128
ROPE_THETA = 10000.0
EPS = 1e-6

LANES = 128
V7X_VMEM_LIMIT_BYTES = 60 * 1024 * 1024


def _params(n_grid_axes):
    return pltpu.CompilerParams(
        dimension_semantics=("arbitrary",) * n_grid_axes,
        vmem_limit_bytes=V7X_VMEM_LIMIT_BYTES,
    )


def _resident(block_shape, index_map):
    return pl.BlockSpec(block_shape, index_map, pipeline_mode=pl.Buffered(1))


def _tiles(total, tile):
    return [(o, tile) for o in range(0, total, tile)]


def _lane_tile(r, width):
    return r if width == LANES else jnp.concatenate([r] * (width // LANES), axis=1)


def _lane_block_sum(x):
    acc = x[:, :LANES]
    for c in range(LANES, x.shape[1], LANES):
        acc = acc + x[:, c:c + LANES]
    return acc


def _rmsnorm_body(x_ref, g_ref, o_ref):
    x = x_ref[...]
    y = x * lax.rsqrt(jnp.mean(x * x, axis=-1, keepdims=True) + EPS)
    o_ref[...] = (y * g_ref[...]).astype(o_ref.dtype)


def _rmsnorm(x, gain, block_rows=512):
    t, d = x.shape
    return pl.pallas_call(
        _rmsnorm_body,
        grid=(t // block_rows,),
        in_specs=[
            pl.BlockSpec((block_rows, d), lambda i: (i, 0)),
            pl.BlockSpec((1, d), lambda i: (0, 0)),
        ],
        out_specs=pl.BlockSpec((block_rows, d), lambda i: (i, 0)),
        out_shape=jax.ShapeDtypeStruct((t, d), F32),
        compiler_params=_params(1),
        name="rmsnorm",
    )(x, gain.reshape(1, d))


def _norm_split_body(x_ref, g_ref, xg_ref, r_ref):
    x = x_ref[...]
    xg_ref[...] = (x * g_ref[...]).astype(xg_ref.dtype)
    r = lax.rsqrt(jnp.mean(x * x, axis=-1, keepdims=True) + EPS)
    r_ref[...] = jnp.broadcast_to(r, r_ref.shape)


def _norm_split(x, gain, block_rows=512):
    t, d = x.shape
    return pl.pallas_call(
        _norm_split_body,
        grid=(t // block_rows,),
        in_specs=[
            pl.BlockSpec((block_rows, d), lambda i: (i, 0)),
            pl.BlockSpec((1, d), lambda i: (0, 0)),
        ],
        out_specs=[
            pl.BlockSpec((block_rows, d), lambda i: (i, 0)),
            pl.BlockSpec((block_rows, LANES), lambda i: (i, 0)),
        ],
        out_shape=[jax.ShapeDtypeStruct((t, d), BF16), jax.ShapeDtypeStruct((t, LANES), F32)],
        compiler_params=_params(1),
        name="norm_split",
    )(x, gain.reshape(1, d))


def _ssq_start(ssq_ref):
    @pl.when(pl.program_id(1) == 0)
    def _():
        ssq_ref[...] = jnp.zeros(ssq_ref.shape, ssq_ref.dtype)


def _emit_scaled_and_ssq(x_new, rows, cols, g_ref, xg_ref, ssq_ref):
    r0, rt = rows
    c0, ct = cols
    xg_ref[r0:r0 + rt, c0:c0 + ct] = (x_new * g_ref[:, c0:c0 + ct]).astype(xg_ref.dtype)
    ssq_ref[r0:r0 + rt, :] += _lane_block_sum(x_new * x_new)


def _ssq_finish(ssq_ref, r_ref, d_model):
    @pl.when(pl.program_id(1) == pl.num_programs(1) - 1)
    def _():
        total = jnp.sum(ssq_ref[...], axis=-1, keepdims=True)
        r = lax.rsqrt(total * (1.0 / d_model) + EPS)
        r_ref[...] = jnp.broadcast_to(r, r_ref.shape)


def _gate_up_body(a_ref, r_ref, wg_ref, wu_ref, wd_ref, o_ref, wd16_ref, *, row_tile, col_tile):
    bm, bn = o_ref.shape
    wd16_ref[...] = wd_ref[...].astype(wd16_ref.dtype)
    wg = wg_ref[...].astype(BF16)
    wu = wu_ref[...].astype(BF16)
    for r0, rt in _tiles(bm, row_tile):
        a = a_ref[r0:r0 + rt, :]
        for c0, ct in _tiles(bn, col_tile):
            scale = _lane_tile(r_ref[r0:r0 + rt, :], ct)
            gate = jnp.dot(a, wg[:, c0:c0 + ct], preferred_element_type=F32) * scale
            up = jnp.dot(a, wu[:, c0:c0 + ct], preferred_element_type=F32) * scale
            o_ref[r0:r0 + rt, c0:c0 + ct] = (jax.nn.silu(gate) * up).astype(o_ref.dtype)


def _gate_up(xg, r, w_gate, w_up, w_down, layer, bm=2048, bn=256, row_tile=512, col_tile=256):
    t, d = xg.shape
    ff = w_gate.shape[2]
    n_i, n_j = t // bm, ff // bn
    wd_rows = ff // (n_i * n_j)
    assert wd_rows * n_i * n_j == ff and wd_rows % 16 == 0
    body = functools.partial(_gate_up_body, row_tile=row_tile, col_tile=col_tile)
    return pl.pallas_call(
        body,
        grid=(n_i, n_j),
        in_specs=[
            pl.BlockSpec((bm, d), lambda i, j: (i, 0)),
            _resident((bm, LANES), lambda i, j: (i, 0)),
            pl.BlockSpec((None, d, bn), lambda i, j: (layer, 0, j)),
            pl.BlockSpec((None, d, bn), lambda i, j: (layer, 0, j)),
            pl.BlockSpec((None, wd_rows, d), lambda i, j: (layer, i * n_j + j, 0)),
        ],
        out_specs=[
            pl.BlockSpec((bm, bn), lambda i, j: (i, j)),
            pl.BlockSpec((wd_rows, d), lambda i, j: (i * n_j + j, 0)),
        ],
        out_shape=[jax.ShapeDtypeStruct((t, ff), BF16), jax.ShapeDtypeStruct((ff, d), BF16)],
        compiler_params=_params(2),
        name="gate_up",
    )(xg, r, w_gate, w_up, w_down)


def _down_body(a_ref, w_ref, res_ref, *rest, d_model, emit_norm, col_tile):
    if emit_norm:
        g_ref, o_ref, xg_ref, r_ref, ssq_ref = rest
        _ssq_start(ssq_ref)
    else:
        (o_ref,) = rest
    bm, bn = o_ref.shape
    a = a_ref[...]
    for c0, ct in _tiles(bn, col_tile):
        x_new = res_ref[:, c0:c0 + ct] + 0.5 * jnp.dot(a, w_ref[:, c0:c0 + ct],
                                                      preferred_element_type=F32)
        o_ref[:, c0:c0 + ct] = x_new
        if emit_norm:
            _emit_scaled_and_ssq(x_new, (0, bm), (c0, ct), g_ref, xg_ref, ssq_ref)
    if emit_norm:
        _ssq_finish(ssq_ref, r_ref, d_model)


def _down(a, w16, resid, next_gain=None, bm=512, bn=512, col_tile=256):
    t, ff = a.shape
    d = w16.shape[1]
    emit_norm = next_gain is not None
    in_specs = [
        pl.BlockSpec((bm, ff), lambda i, j: (i, 0)),
        pl.BlockSpec((ff, bn), lambda i, j: (0, j)),
        pl.BlockSpec((bm, bn), lambda i, j: (i, j)),
    ]
    out_specs = [pl.BlockSpec((bm, bn), lambda i, j: (i, j))]
    out_shape = [jax.ShapeDtypeStruct((t, d), F32)]
    args = [a, w16, resid]
    scratch = []
    if emit_norm:
        in_specs.append(pl.BlockSpec((1, bn), lambda i, j: (0, j)))
        out_specs += [pl.BlockSpec((bm, bn), lambda i, j: (i, j)),
                      pl.BlockSpec((bm, LANES), lambda i, j: (i, 0))]
        out_shape += [jax.ShapeDtypeStruct((t, d), BF16), jax.ShapeDtypeStruct((t, LANES), F32)]
        args.append(next_gain.reshape(1, d))
        scratch = [pltpu.VMEM((bm, LANES), F32)]
    out = pl.pallas_call(
        functools.partial(_down_body, d_model=d, emit_norm=emit_norm, col_tile=col_tile),
        grid=(t // bm, d // bn),
        in_specs=in_specs,
        out_specs=out_specs,
        out_shape=out_shape,
        scratch_shapes=scratch,
        compiler_params=_params(2),
        name="down",
    )(*args)
    return out if emit_norm else out[0]


def _in_proj_body(a_ref, r_ref, w_ref, o_ref, *, row_tile, col_tile):
    bm, bn = o_ref.shape
    w = w_ref[...].astype(BF16)
    for r0, rt in _tiles(bm, row_tile):
        a = a_ref[r0:r0 + rt, :]
        for c0, ct in _tiles(bn, col_tile):
            scale = _lane_tile(r_ref[r0:r0 + rt, :], ct)
            o_ref[r0:r0 + rt, c0:c0 + ct] = (
                jnp.dot(a, w[:, c0:c0 + ct], preferred_element_type=F32) * scale)


def _in_proj(xg, r, w, layer, bm=2048, bn=512, row_tile=512, col_tile=256):
    t, d = xg.shape
    n = w.shape[2]
    body = functools.partial(_in_proj_body, row_tile=row_tile, col_tile=col_tile)
    return pl.pallas_call(
        body,
        grid=(t // bm, n // bn),
        in_specs=[
            _resident((bm, d), lambda i, j: (i, 0)),
            _resident((bm, LANES), lambda i, j: (i, 0)),
            pl.BlockSpec((None, d, bn), lambda i, j: (layer, 0, j)),
        ],
        out_specs=pl.BlockSpec((bm, bn), lambda i, j: (i, j)),
        out_shape=jax.ShapeDtypeStruct((t, n), F32),
        compiler_params=_params(2),
        name="in_proj",
    )(xg, r, w)


def _out_proj_body(a1_ref, a2_ref, w1_ref, w2_ref, res_ref, g_ref, o_ref, xg_ref, r_ref, ssq_ref,
                   *, row_tile, col_tile, d_model):
    bm, bn = o_ref.shape
    _ssq_start(ssq_ref)
    w1 = w1_ref[...].astype(BF16)
    w2 = w2_ref[...].astype(BF16)
    for r0, rt in _tiles(bm, row_tile):
        a1 = a1_ref[r0:r0 + rt, :]
        a2 = a2_ref[r0:r0 + rt, :]
        for c0, ct in _tiles(bn, col_tile):
            acc = (jnp.dot(a1, w1[:, c0:c0 + ct], preferred_element_type=F32)
                   + jnp.dot(a2, w2[:, c0:c0 + ct], preferred_element_type=F32))
            x_new = res_ref[r0:r0 + rt, c0:c0 + ct] + acc
            o_ref[r0:r0 + rt, c0:c0 + ct] = x_new
            _emit_scaled_and_ssq(x_new, (r0, rt), (c0, ct), g_ref, xg_ref, ssq_ref)
    _ssq_finish(ssq_ref, r_ref, d_model)


def _out_proj(a_ret, a_conv, w, resid, layer, next_gain, bm=2048, bn=512, row_tile=1024,
              col_tile=256):
    t, d_ret = a_ret.shape
    d_conv = a_conv.shape[1]
    assert d_ret == d_conv
    n = w.shape[2]
    body = functools.partial(_out_proj_body, row_tile=row_tile, col_tile=col_tile, d_model=n)
    return pl.pallas_call(
        body,
        grid=(t // bm, n // bn),
        in_specs=[
            _resident((bm, d_ret), lambda i, j: (i, 0)),
            _resident((bm, d_conv), lambda i, j: (i, 0)),
            pl.BlockSpec((None, d_ret, bn), lambda i, j: (layer, 0, j)),
            pl.BlockSpec((None, d_conv, bn), lambda i, j: (layer, 1, j)),
            pl.BlockSpec((bm, bn), lambda i, j: (i, j)),
            pl.BlockSpec((1, bn), lambda i, j: (0, j)),
        ],
        out_specs=[
            pl.BlockSpec((bm, bn), lambda i, j: (i, j)),
            pl.BlockSpec((bm, bn), lambda i, j: (i, j)),
            pl.BlockSpec((bm, LANES), lambda i, j: (i, 0)),
        ],
        out_shape=[jax.ShapeDtypeStruct((t, n), F32), jax.ShapeDtypeStruct((t, n), BF16),
                   jax.ShapeDtypeStruct((t, LANES), F32)],
        scratch_shapes=[pltpu.VMEM((bm, LANES), F32)],
        compiler_params=_params(2),
        name="out_proj",
    )(a_ret, a_conv, w, w, resid, next_gain.reshape(1, n))


def _block_diag(top_left, bottom_right):
    zero = jnp.zeros(top_left.shape, top_left.dtype)
    return jnp.concatenate([jnp.concatenate([top_left, zero], axis=1),
                            jnp.concatenate([zero, bottom_right], axis=1)], axis=0)


def _retention_body(q_ref, k_ref, v_ref, g_ref, cos_ref, sin_ref, lg_ref, gn_ref, o_ref):
    s, width = q_ref.shape
    dh = HEAD_DIM
    assert width == 2 * dh and CHUNK == dh
    n_chunks = s // CHUNK
    lg = lg_ref[...]

    row = lax.broadcasted_iota(jnp.int32, (CHUNK, width), 0)
    key = lax.broadcasted_iota(jnp.int32, (CHUNK, width), 1) % CHUNK
    diff = (row - key).astype(F32)
    intra = jnp.where(diff >= 0, jnp.exp(lg * jnp.maximum(diff, 0.0)), 0.0)
    idx = row.astype(F32)
    zeta = jnp.exp(lg * (CHUNK - 1.0 - idx))
    xi = jnp.exp(lg * (idx + 1.0))
    chunk_decay = jnp.exp(lg * float(CHUNK))
    heads = (slice(0, dh), slice(dh, width))

    def rope(x, cos, sin):
        return jnp.concatenate(
            [x[:, h] * cos + pltpu.roll(x[:, h], dh // 2, 1) * sin for h in heads], axis=1)

    states = [jnp.zeros((dh, dh), F32) for _ in heads]
    for c in range(n_chunks):
        rows = pl.ds(c * CHUNK, CHUNK)
        cos = cos_ref[rows, :]
        sin = sin_ref[rows, :]
        q = rope(q_ref[rows, :], cos, sin)
        k = rope(k_ref[rows, :], cos, sin) * (dh ** -0.5)
        q16 = q.astype(BF16)
        k16 = k.astype(BF16)
        v16 = v_ref[rows, :].astype(BF16)

        scores = lax.dot_general(q16, _block_diag(k16[:, heads[0]], k16[:, heads[1]]),
                                 (((1,), (1,)), ((), ())), preferred_element_type=F32) * intra
        lhs = jnp.concatenate([scores.astype(BF16), (q * xi).astype(BF16)], axis=1)
        rhs = jnp.concatenate(
            [_block_diag(v16[:, heads[0]], v16[:, heads[1]]),
             _block_diag(states[0].astype(BF16), states[1].astype(BF16))], axis=0)
        out = jnp.dot(lhs, rhs, preferred_element_type=F32)
        kv = lax.dot_general((k * zeta).astype(BF16), v16, (((0,), (0,)), ((), ())),
                             preferred_element_type=F32)
        for i, h in enumerate(heads):
            states[i] = states[i] * chunk_decay[:, h] + kv[h, h]
            o = out[:, h]
            o = o * lax.rsqrt(jnp.mean(o * o, axis=-1, keepdims=True) + EPS)
            o = o * gn_ref[:, h]
            o_ref[rows, h] = (jax.nn.silu(g_ref[rows, h]) * o).astype(o_ref.dtype)


def _rope_tables(seq, dh):
    half = dh // 2
    inv_freq = ROPE_THETA ** (-jnp.arange(half, dtype=F32) / half)
    ang = jnp.arange(seq, dtype=F32)[:, None] * inv_freq[None, :]
    cos = jnp.cos(ang)
    sin = jnp.sin(ang)
    return jnp.concatenate([cos, cos], axis=-1), jnp.concatenate([-sin, sin], axis=-1)


def _retention(proj, ret_norm, heads_per_step=2):
    b, s, _ = proj.shape
    h, dh = RET_HEADS, HEAD_DIM
    width = heads_per_step * dh
    steps = h // heads_per_step
    cos, sin = _rope_tables(s, dh)
    log_gamma = jnp.log1p(-jnp.power(2.0, -5.0 - jnp.arange(h, dtype=F32)))
    lg = jnp.repeat(log_gamma, dh).reshape(1, h * dh)
    gn = ret_norm.astype(F32).reshape(1, h * dh)

    def head_cols(group):
        return pl.BlockSpec((None, s, width), lambda bi, hi: (bi, 0, group * steps + hi))

    table = pl.BlockSpec((s, dh), lambda bi, hi: (0, 0))
    per_head = pl.BlockSpec((1, width), lambda bi, hi: (0, hi))
    return pl.pallas_call(
        _retention_body,
        grid=(b, steps),
        in_specs=[head_cols(0), head_cols(1), head_cols(2), head_cols(3),
                  table, table, per_head, per_head],
        out_specs=pl.BlockSpec((None, s, width), lambda bi, hi: (bi, 0, hi)),
        out_shape=jax.ShapeDtypeStruct((b, s, h * dh), BF16),
        compiler_params=_params(2),
        name="retention",
    )(proj, proj, proj, proj, cos, sin, lg, gn)


def _short_conv_body(b_ref, c_ref, u_ref, w_ref, o_ref):
    cu = c_ref[...] * u_ref[...]
    t = lax.broadcasted_iota(jnp.int32, cu.shape, 0)
    acc = w_ref[CONV_WIDTH - 1:CONV_WIDTH, :] * cu
    for lag in range(1, CONV_WIDTH):
        shifted = jnp.where(t >= lag, pltpu.roll(cu, lag, 0), 0.0)
        acc = acc + w_ref[CONV_WIDTH - 1 - lag:CONV_WIDTH - lag, :] * shifted
    o_ref[...] = (b_ref[...] * acc).astype(o_ref.dtype)


def _short_conv(proj, conv_w, channels_per_step=512):
    b, s, _ = proj.shape
    d_conv = conv_w.shape[1]
    steps = d_conv // channels_per_step
    first = 4 * RET_HEADS * HEAD_DIM // channels_per_step

    def block_cols(which):
        return pl.BlockSpec((None, s, channels_per_step),
                            lambda bi, ci: (bi, 0, first + which * steps + ci))

    return pl.pallas_call(
        _short_conv_body,
        grid=(b, steps),
        in_specs=[block_cols(0), block_cols(1), block_cols(2),
                  pl.BlockSpec((CONV_WIDTH, channels_per_step), lambda bi, ci: (0, ci))],
        out_specs=pl.BlockSpec((None, s, channels_per_step), lambda bi, ci: (bi, 0, ci)),
        out_shape=jax.ShapeDtypeStruct((b, s, d_conv), BF16),
        compiler_params=_params(2),
        name="short_conv",
    )(proj, proj, proj, conv_w)


def kernel(x, norm_ffa, w_ffa_gate, w_ffa_up, w_ffa_down, norm_mix, w_in, conv_w, ret_norm,
           w_out, norm_ffb, w_ffb_gate, w_ffb_up, w_ffb_down, norm_final):
    batch, seq, d_model = x.shape
    depth = norm_ffa.shape[0]
    x = x.reshape(batch * seq, d_model)
    xg, r = _norm_split(x, norm_ffa[0])
    for l in range(depth):
        hidden, w_down16 = _gate_up(xg, r, w_ffa_gate, w_ffa_up, w_ffa_down, l)
        x, xg, r = _down(hidden, w_down16, x, next_gain=norm_mix[l])
        proj = _in_proj(xg, r, w_in, l).reshape(batch, seq, -1)
        ret_out = _retention(proj, ret_norm[l]).reshape(batch * seq, -1)
        conv_out = _short_conv(proj, conv_w[l]).reshape(batch * seq, -1)
        x, xg, r = _out_proj(ret_out, conv_out, w_out, x, l, norm_ffb[l])
        hidden, w_down16 = _gate_up(xg, r, w_ffb_gate, w_ffb_up, w_ffb_down, l)
        if l + 1 < depth:
            x, xg, r = _down(hidden, w_down16, x, next_gain=norm_ffa[l + 1])
        else:
            x = _down(hidden, w_down16, x)
    return _rmsnorm(x, norm_final).reshape(batch, seq, d_model)
```

```python
import functools

import jax
import jax.numpy as jnp
from jax import lax
from jax.experimental import pallas as pl
from jax.experimental.pallas import tpu as pltpu

F32 = jnp.float32
BF16 = jnp.bfloat16

RET_HEADS = 16
HEAD_DIM = 128
CONV_WIDTH = 3
CHUNK = 128
ROPE_THETA = 10000.0
EPS = 1e-6

LANES = 128
V7X_VMEM_LIMIT_BYTES = 60 * 1024 * 1024

GATE_UP_TILES = dict(bm=2048, bn=256, row_tile=512, col_tile=256)
DOWN_TILES = dict(bm=512, bn=512, col_tile=256)
IN_PROJ_TILES = dict(bm=2048, bn=512, row_tile=512, col_tile=256)
OUT_PROJ_TILES = dict(bm=2048, bn=512, row_tile=1024, col_tile=256)
NORM_BLOCK_ROWS = 512
RETENTION_HEADS_PER_STEP = 2
CONV_CHANNELS_PER_STEP = 512


def _params(n_grid_axes):
    return pltpu.CompilerParams(
        dimension_semantics=("arbitrary",) * n_grid_axes,
        vmem_limit_bytes=V7X_VMEM_LIMIT_BYTES,
    )


def _resident(block_shape, index_map):
    return pl.BlockSpec(block_shape, index_map, pipeline_mode=pl.Buffered(1))


def _tiles(total, tile):
    return [(o, tile) for o in range(0, total, tile)]


def _lane_tile(r, width):
    return r if width == LANES else jnp.concatenate([r] * (width // LANES), axis=1)


def _lane_block_sum(x):
    acc = x[:, :LANES]
    for c in range(LANES, x.shape[1], LANES):
        acc = acc + x[:, c:c + LANES]
    return acc


def _rmsnorm_body(x_ref, g_ref, o_ref):
    x = x_ref[...]
    y = x * lax.rsqrt(jnp.mean(x * x, axis=-1, keepdims=True) + EPS)
    o_ref[...] = (y * g_ref[...]).astype(o_ref.dtype)


def _rmsnorm(x, gain, block_rows=NORM_BLOCK_ROWS):
    t, d = x.shape
    return pl.pallas_call(
        _rmsnorm_body,
        grid=(t // block_rows,),
        in_specs=[
            pl.BlockSpec((block_rows, d), lambda i: (i, 0)),
            pl.BlockSpec((1, d), lambda i: (0, 0)),
        ],
        out_specs=pl.BlockSpec((block_rows, d), lambda i: (i, 0)),
        out_shape=jax.ShapeDtypeStruct((t, d), F32),
        compiler_params=_params(1),
        name="rmsnorm",
    )(x, gain.reshape(1, d))


def _norm_split_body(x_ref, g_ref, xg_ref, r_ref):
    x = x_ref[...]
    xg_ref[...] = (x * g_ref[...]).astype(xg_ref.dtype)
    r = lax.rsqrt(jnp.mean(x * x, axis=-1, keepdims=True) + EPS)
    r_ref[...] = jnp.broadcast_to(r, r_ref.shape)


def _norm_split(x, gain, block_rows=NORM_BLOCK_ROWS):
    t, d = x.shape
    return pl.pallas_call(
        _norm_split_body,
        grid=(t // block_rows,),
        in_specs=[
            pl.BlockSpec((block_rows, d), lambda i: (i, 0)),
            pl.BlockSpec((1, d), lambda i: (0, 0)),
        ],
        out_specs=[
            pl.BlockSpec((block_rows, d), lambda i: (i, 0)),
            pl.BlockSpec((block_rows, LANES), lambda i: (i, 0)),
        ],
        out_shape=[jax.ShapeDtypeStruct((t, d), BF16), jax.ShapeDtypeStruct((t, LANES), F32)],
        compiler_params=_params(1),
        name="norm_split",
    )(x, gain.reshape(1, d))


def _ssq_start(ssq_ref):
    @pl.when(pl.program_id(1) == 0)
    def _():
        ssq_ref[...] = jnp.zeros(ssq_ref.shape, ssq_ref.dtype)


def _emit_scaled_and_ssq(x_new, rows, cols, g_ref, xg_ref, ssq_ref):
    r0, rt = rows
    c0, ct = cols
    xg_ref[r0:r0 + rt, c0:c0 + ct] = (x_new * g_ref[:, c0:c0 + ct]).astype(xg_ref.dtype)
    ssq_ref[r0:r0 + rt, :] += _lane_block_sum(x_new * x_new)


def _ssq_finish(ssq_ref, r_ref, d_model):
    @pl.when(pl.program_id(1) == pl.num_programs(1) - 1)
    def _():
        total = jnp.sum(ssq_ref[...], axis=-1, keepdims=True)
        r = lax.rsqrt(total * (1.0 / d_model) + EPS)
        r_ref[...] = jnp.broadcast_to(r, r_ref.shape)


def _gate_up_body(a_ref, r_ref, wg_ref, wu_ref, wd_ref, o_ref, wd16_ref, *, row_tile, col_tile):
    bm, bn = o_ref.shape
    wd16_ref[...] = wd_ref[...].astype(wd16_ref.dtype)
    wg = wg_ref[...].astype(BF16)
    wu = wu_ref[...].astype(BF16)
    for r0, rt in _tiles(bm, row_tile):
        a = a_ref[r0:r0 + rt, :]
        for c0, ct in _tiles(bn, col_tile):
            scale = _lane_tile(r_ref[r0:r0 + rt, :], ct)
            gate = jnp.dot(a, wg[:, c0:c0 + ct], preferred_element_type=F32) * scale
            up = jnp.dot(a, wu[:, c0:c0 + ct], preferred_element_type=F32) * scale
            o_ref[r0:r0 + rt, c0:c0 + ct] = (jax.nn.silu(gate) * up).astype(o_ref.dtype)


def _gate_up(xg, r, w_gate, w_up, w_down, layer, *, bm, bn, row_tile, col_tile):
    t, d = xg.shape
    ff = w_gate.shape[2]
    n_i, n_j = t // bm, ff // bn
    wd_rows = ff // (n_i * n_j)
    assert wd_rows * n_i * n_j == ff and wd_rows % 16 == 0
    body = functools.partial(_gate_up_body, row_tile=row_tile, col_tile=col_tile)
    return pl.pallas_call(
        body,
        grid=(n_i, n_j),
        in_specs=[
            pl.BlockSpec((bm, d), lambda i, j: (i, 0)),
            _resident((bm, LANES), lambda i, j: (i, 0)),
            pl.BlockSpec((None, d, bn), lambda i, j: (layer, 0, j)),
            pl.BlockSpec((None, d, bn), lambda i, j: (layer, 0, j)),
            pl.BlockSpec((None, wd_rows, d), lambda i, j: (layer, i * n_j + j, 0)),
        ],
        out_specs=[
            pl.BlockSpec((bm, bn), lambda i, j: (i, j)),
            pl.BlockSpec((wd_rows, d), lambda i, j: (i * n_j + j, 0)),
        ],
        out_shape=[jax.ShapeDtypeStruct((t, ff), BF16), jax.ShapeDtypeStruct((ff, d), BF16)],
        compiler_params=_params(2),
        name="gate_up",
    )(xg, r, w_gate, w_up, w_down)


def _down_body(a_ref, w_ref, res_ref, *rest, d_model, emit_norm, col_tile):
    if emit_norm:
        g_ref, o_ref, xg_ref, r_ref, ssq_ref = rest
        _ssq_start(ssq_ref)
    else:
        (o_ref,) = rest
    bm, bn = o_ref.shape
    a = a_ref[...]
    for c0, ct in _tiles(bn, col_tile):
        x_new = res_ref[:, c0:c0 + ct] + 0.5 * jnp.dot(a, w_ref[:, c0:c0 + ct],
                                                      preferred_element_type=F32)
        o_ref[:, c0:c0 + ct] = x_new
        if emit_norm:
            _emit_scaled_and_ssq(x_new, (0, bm), (c0, ct), g_ref, xg_ref, ssq_ref)
    if emit_norm:
        _ssq_finish(ssq_ref, r_ref, d_model)


def _down(a, w16, resid, next_gain=None, *, bm, bn, col_tile):
    t, ff = a.shape
    d = w16.shape[1]
    emit_norm = next_gain is not None
    in_specs = [
        pl.BlockSpec((bm, ff), lambda i, j: (i, 0)),
        pl.BlockSpec((ff, bn), lambda i, j: (0, j)),
        pl.BlockSpec((bm, bn), lambda i, j: (i, j)),
    ]
    out_specs = [pl.BlockSpec((bm, bn), lambda i, j: (i, j))]
    out_shape = [jax.ShapeDtypeStruct((t, d), F32)]
    args = [a, w16, resid]
    scratch = []
    if emit_norm:
        in_specs.append(pl.BlockSpec((1, bn), lambda i, j: (0, j)))
        out_specs += [pl.BlockSpec((bm, bn), lambda i, j: (i, j)),
                      pl.BlockSpec((bm, LANES), lambda i, j: (i, 0))]
        out_shape += [jax.ShapeDtypeStruct((t, d), BF16), jax.ShapeDtypeStruct((t, LANES), F32)]
        args.append(next_gain.reshape(1, d))
        scratch = [pltpu.VMEM((bm, LANES), F32)]
    out = pl.pallas_call(
        functools.partial(_down_body, d_model=d, emit_norm=emit_norm, col_tile=col_tile),
        grid=(t // bm, d // bn),
        in_specs=in_specs,
        out_specs=out_specs,
        out_shape=out_shape,
        scratch_shapes=scratch,
        compiler_params=_params(2),
        name="down",
    )(*args)
    return out if emit_norm else out[0]


def _in_proj_body(a_ref, r_ref, w_ref, o_ref, *, row_tile, col_tile):
    bm, bn = o_ref.shape
    w = w_ref[...].astype(BF16)
    for r0, rt in _tiles(bm, row_tile):
        a = a_ref[r0:r0 + rt, :]
        for c0, ct in _tiles(bn, col_tile):
            scale = _lane_tile(r_ref[r0:r0 + rt, :], ct)
            o_ref[r0:r0 + rt, c0:c0 + ct] = (
                jnp.dot(a, w[:, c0:c0 + ct], preferred_element_type=F32) * scale)


def _in_proj(xg, r, w, layer, *, bm, bn, row_tile, col_tile):
    t, d = xg.shape
    n = w.shape[2]
    body = functools.partial(_in_proj_body, row_tile=row_tile, col_tile=col_tile)
    return pl.pallas_call(
        body,
        grid=(t // bm, n // bn),
        in_specs=[
            _resident((bm, d), lambda i, j: (i, 0)),
            _resident((bm, LANES), lambda i, j: (i, 0)),
            pl.BlockSpec((None, d, bn), lambda i, j: (layer, 0, j)),
        ],
        out_specs=pl.BlockSpec((bm, bn), lambda i, j: (i, j)),
        out_shape=jax.ShapeDtypeStruct((t, n), F32),
        compiler_params=_params(2),
        name="in_proj",
    )(xg, r, w)


def _out_proj_body(a1_ref, a2_ref, w1_ref, w2_ref, res_ref, g_ref, o_ref, xg_ref, r_ref, ssq_ref,
                   *, row_tile, col_tile, d_model):
    bm, bn = o_ref.shape
    _ssq_start(ssq_ref)
    w1 = w1_ref[...].astype(BF16)
    w2 = w2_ref[...].astype(BF16)
    for r0, rt in _tiles(bm, row_tile):
        a1 = a1_ref[r0:r0 + rt, :]
        a2 = a2_ref[r0:r0 + rt, :]
        for c0, ct in _tiles(bn, col_tile):
            acc = (jnp.dot(a1, w1[:, c0:c0 + ct], preferred_element_type=F32)
                   + jnp.dot(a2, w2[:, c0:c0 + ct], preferred_element_type=F32))
            x_new = res_ref[r0:r0 + rt, c0:c0 + ct] + acc
            o_ref[r0:r0 + rt, c0:c0 + ct] = x_new
            _emit_scaled_and_ssq(x_new, (r0, rt), (c0, ct), g_ref, xg_ref, ssq_ref)
    _ssq_finish(ssq_ref, r_ref, d_model)


def _out_proj(a_ret, a_conv, w, resid, layer, next_gain, *, bm, bn, row_tile, col_tile):
    t, d_ret = a_ret.shape
    d_conv = a_conv.shape[1]
    assert d_ret == d_conv
    n = w.shape[2]
    body = functools.partial(_out_proj_body, row_tile=row_tile, col_tile=col_tile, d_model=n)
    return pl.pallas_call(
        body,
        grid=(t // bm, n // bn),
        in_specs=[
            _resident((bm, d_ret), lambda i, j: (i, 0)),
            _resident((bm, d_conv), lambda i, j: (i, 0)),
            pl.BlockSpec((None, d_ret, bn), lambda i, j: (layer, 0, j)),
            pl.BlockSpec((None, d_conv, bn), lambda i, j: (layer, 1, j)),
            pl.BlockSpec((bm, bn), lambda i, j: (i, j)),
            pl.BlockSpec((1, bn), lambda i, j: (0, j)),
        ],
        out_specs=[
            pl.BlockSpec((bm, bn), lambda i, j: (i, j)),
            pl.BlockSpec((bm, bn), lambda i, j: (i, j)),
            pl.BlockSpec((bm, LANES), lambda i, j: (i, 0)),
        ],
        out_shape=[jax.ShapeDtypeStruct((t, n), F32), jax.ShapeDtypeStruct((t, n), BF16),
                   jax.ShapeDtypeStruct((t, LANES), F32)],
        scratch_shapes=[pltpu.VMEM((bm, LANES), F32)],
        compiler_params=_params(2),
        name="out_proj",
    )(a_ret, a_conv, w, w, resid, next_gain.reshape(1, n))


def _block_diag(top_left, bottom_right):
    zero = jnp.zeros(top_left.shape, top_left.dtype)
    return jnp.concatenate([jnp.concatenate([top_left, zero], axis=1),
                            jnp.concatenate([zero, bottom_right], axis=1)], axis=0)


def _retention_body(q_ref, k_ref, v_ref, g_ref, cos_ref, sin_ref, lg_ref, gn_ref, o_ref):
    s, width = q_ref.shape
    dh = HEAD_DIM
    assert width == 2 * dh and CHUNK == dh
    n_chunks = s // CHUNK
    lg = lg_ref[...]

    row = lax.broadcasted_iota(jnp.int32, (CHUNK, width), 0)
    key = lax.broadcasted_iota(jnp.int32, (CHUNK, width), 1) % CHUNK
    diff = (row - key).astype(F32)
    intra = jnp.where(diff >= 0, jnp.exp(lg * jnp.maximum(diff, 0.0)), 0.0)
    idx = row.astype(F32)
    zeta = jnp.exp(lg * (CHUNK - 1.0 - idx))
    xi = jnp.exp(lg * (idx + 1.0))
    chunk_decay = jnp.exp(lg * float(CHUNK))
    heads = (slice(0, dh), slice(dh, width))

    def rope(x, cos, sin):
        return jnp.concatenate(
            [x[:, h] * cos + pltpu.roll(x[:, h], dh // 2, 1) * sin for h in heads], axis=1)

    states = [jnp.zeros((dh, dh), F32) for _ in heads]
    for c in range(n_chunks):
        rows = pl.ds(c * CHUNK, CHUNK)
        cos = cos_ref[rows, :]
        sin = sin_ref[rows, :]
        q = rope(q_ref[rows, :], cos, sin)
        k = rope(k_ref[rows, :], cos, sin) * (dh ** -0.5)
        q16 = q.astype(BF16)
        k16 = k.astype(BF16)
        v16 = v_ref[rows, :].astype(BF16)

        scores = lax.dot_general(q16, _block_diag(k16[:, heads[0]], k16[:, heads[1]]),
                                 (((1,), (1,)), ((), ())), preferred_element_type=F32) * intra
        lhs = jnp.concatenate([scores.astype(BF16), (q * xi).astype(BF16)], axis=1)
        rhs = jnp.concatenate(
            [_block_diag(v16[:, heads[0]], v16[:, heads[1]]),
             _block_diag(states[0].astype(BF16), states[1].astype(BF16))], axis=0)
        out = jnp.dot(lhs, rhs, preferred_element_type=F32)
        kv = lax.dot_general((k * zeta).astype(BF16), v16, (((0,), (0,)), ((), ())),
                             preferred_element_type=F32)
        for i, h in enumerate(heads):
            states[i] = states[i] * chunk_decay[:, h] + kv[h, h]
            o = out[:, h]
            o = o * lax.rsqrt(jnp.mean(o * o, axis=-1, keepdims=True) + EPS)
            o = o * gn_ref[:, h]
            o_ref[rows, h] = (jax.nn.silu(g_ref[rows, h]) * o).astype(o_ref.dtype)


def _rope_tables(seq, dh):
    half = dh // 2
    inv_freq = ROPE_THETA ** (-jnp.arange(half, dtype=F32) / half)
    ang = jnp.arange(seq, dtype=F32)[:, None] * inv_freq[None, :]
    cos = jnp.cos(ang)
    sin = jnp.sin(ang)
    return jnp.concatenate([cos, cos], axis=-1), jnp.concatenate([-sin, sin], axis=-1)


def _retention(proj, ret_norm, heads_per_step=RETENTION_HEADS_PER_STEP):
    b, s, _ = proj.shape
    h, dh = RET_HEADS, HEAD_DIM
    width = heads_per_step * dh
    steps = h // heads_per_step
    cos, sin = _rope_tables(s, dh)
    log_gamma = jnp.log1p(-jnp.power(2.0, -5.0 - jnp.arange(h, dtype=F32)))
    lg = jnp.repeat(log_gamma, dh).reshape(1, h * dh)
    gn = ret_norm.astype(F32).reshape(1, h * dh)

    def head_cols(group):
        return pl.BlockSpec((None, s, width), lambda bi, hi: (bi, 0, group * steps + hi))

    table = pl.BlockSpec((s, dh), lambda bi, hi: (0, 0))
    per_head = pl.BlockSpec((1, width), lambda bi, hi: (0, hi))
    return pl.pallas_call(
        _retention_body,
        grid=(b, steps),
        in_specs=[head_cols(0), head_cols(1), head_cols(2), head_cols(3),
                  table, table, per_head, per_head],
        out_specs=pl.BlockSpec((None, s, width), lambda bi, hi: (bi, 0, hi)),
        out_shape=jax.ShapeDtypeStruct((b, s, h * dh), BF16),
        compiler_params=_params(2),
        name="retention",
    )(proj, proj, proj, proj, cos, sin, lg, gn)


def _short_conv_body(b_ref, c_ref, u_ref, w_ref, o_ref):
    cu = c_ref[...] * u_ref[...]
    t = lax.broadcasted_iota(jnp.int32, cu.shape, 0)
    acc = w_ref[CONV_WIDTH - 1:CONV_WIDTH, :] * cu
    for lag in range(1, CONV_WIDTH):
        shifted = jnp.where(t >= lag, pltpu.roll(cu, lag, 0), 0.0)
        acc = acc + w_ref[CONV_WIDTH - 1 - lag:CONV_WIDTH - lag, :] * shifted
    o_ref[...] = (b_ref[...] * acc).astype(o_ref.dtype)


def _short_conv(proj, conv_w, channels_per_step=CONV_CHANNELS_PER_STEP):
    b, s, _ = proj.shape
    d_conv = conv_w.shape[1]
    steps = d_conv // channels_per_step
    first = 4 * RET_HEADS * HEAD_DIM // channels_per_step

    def block_cols(which):
        return pl.BlockSpec((None, s, channels_per_step),
                            lambda bi, ci: (bi, 0, first + which * steps + ci))

    return pl.pallas_call(
        _short_conv_body,
        grid=(b, steps),
        in_specs=[block_cols(0), block_cols(1), block_cols(2),
                  pl.BlockSpec((CONV_WIDTH, channels_per_step), lambda bi, ci: (0, ci))],
        out_specs=pl.BlockSpec((None, s, channels_per_step), lambda bi, ci: (bi, 0, ci)),
        out_shape=jax.ShapeDtypeStruct((b, s, d_conv), BF16),
        compiler_params=_params(2),
        name="short_conv",
    )(proj, proj, proj, conv_w)


def kernel(x, norm_ffa, w_ffa_gate, w_ffa_up, w_ffa_down, norm_mix, w_in, conv_w, ret_norm,
           w_out, norm_ffb, w_ffb_gate, w_ffb_up, w_ffb_down, norm_final):
    batch, seq, d_model = x.shape
    depth = norm_ffa.shape[0]
    x = x.reshape(batch * seq, d_model)
    xg, r = _norm_split(x, norm_ffa[0])
    for l in range(depth):
        hidden, w_down16 = _gate_up(xg, r, w_ffa_gate, w_ffa_up, w_ffa_down, l, **GATE_UP_TILES)
        x, xg, r = _down(hidden, w_down16, x, next_gain=norm_mix[l], **DOWN_TILES)
        proj = _in_proj(xg, r, w_in, l, **IN_PROJ_TILES).reshape(batch, seq, -1)
        ret_out = _retention(proj, ret_norm[l]).reshape(batch * seq, -1)
        conv_out = _short_conv(proj, conv_w[l]).reshape(batch * seq, -1)
        x, xg, r = _out_proj(ret_out, conv_out, w_out, x, l, norm_ffb[l], **OUT_PROJ_TILES)
        hidden, w_down16 = _gate_up(xg, r, w_ffb_gate, w_ffb_up, w_ffb_down, l, **GATE_UP_TILES)
        if l + 1 < depth:
            x, xg, r = _down(hidden, w_down16, x, next_gain=norm_ffa[l + 1], **DOWN_TILES)
        else:
            x = _down(hidden, w_down16, x, **DOWN_TILES)
    return _rmsnorm(x, norm_final).reshape(batch, seq, d_model)
```
